```python
import math
import jax, jax.numpy as jnp
from jax import lax
import numpy as np

D_MODEL = 4096
BATCH = 2
SEQ = 8192
DEPTH = 1
DEC_BATCH = 8
DEC_SEQ = 32
PAST_LEN = 1024

CHUNK = 64
N_HEADS = 16
HEAD_DIM = 128
V_DIM = 2 * HEAD_DIM
ATTN_Q = N_HEADS * 2 * HEAD_DIM
ATTN_K = N_HEADS * 2 * HEAD_DIM
ATTN_V = N_HEADS * V_DIM
POOL_WINDOWS = (2, 4, 8, 16)
N_POOL_GROUPS = len(POOL_WINDOWS)
POOL_GROUP_IN = D_MODEL // 8
POOL_IN = N_POOL_GROUPS * POOL_GROUP_IN
POOL_GROUP_OUT = D_MODEL // N_POOL_GROUPS
POOL_HIST = max(POOL_WINDOWS) - 1
D_FF = -(-(8 * D_MODEL) // (3 * 256)) * 256
IN_COLS = POOL_IN + ATTN_Q + ATTN_K + ATTN_V + 2 * D_MODEL
QBLOCK = 128
EPS = 1e-6

kernel_name = 'gated_pool_diffattn_streaming_step'


def rms_norm(x, g):
    xf = x.astype(jnp.float32)
    y = xf * lax.rsqrt(jnp.mean(xf * xf, axis=-1, keepdims=True) + EPS) * g.astype(jnp.float32)
    return y.astype(x.dtype)


def lambda_init(layer):
    return 0.8 - 0.6 * math.exp(-0.3 * layer)


def project(x, norm_g, w_in, q_g, k_g):
    B, T = x.shape[0], x.shape[1]
    h = rms_norm(x, norm_g)
    z = h @ w_in
    o1 = POOL_IN
    o2 = o1 + ATTN_Q
    o3 = o2 + ATTN_K
    o4 = o3 + ATTN_V
    o5 = o4 + D_MODEL
    u = z[..., :o1]
    q = rms_norm(z[..., o1:o2].reshape(B, T, N_HEADS, 2, HEAD_DIM), q_g)
    k = rms_norm(z[..., o2:o3].reshape(B, T, N_HEADS, 2, HEAD_DIM), k_g)
    v = z[..., o3:o4].reshape(B, T, N_HEADS, V_DIM)
    return u, q, k, v, z[..., o4:o5], z[..., o5:]


def pool_mix(u_ext, pos, w_pool, pool_scale):
    B = u_ext.shape[0]
    T = pos.shape[0]
    uf = u_ext.astype(jnp.float32)
    cs = jnp.pad(jnp.cumsum(uf, axis=1), ((0, 0), (1, 0), (0, 0)))
    u_new = uf[:, POOL_HIST:]
    outs = []
    for g, w in enumerate(POOL_WINDOWS):
        c0, c1 = g * POOL_GROUP_IN, (g + 1) * POOL_GROUP_IN
        hi = cs[:, POOL_HIST + 1:POOL_HIST + 1 + T, c0:c1]
        lo = cs[:, POOL_HIST + 1 - w:POOL_HIST + 1 - w + T, c0:c1]
        cnt = jnp.minimum(pos + 1, w).astype(jnp.float32)
        outs.append((hi - lo) / cnt[None, :, None] - u_new[:, :, c0:c1])
    pooled = jnp.stack(outs, axis=2).astype(u_ext.dtype)
    y = jnp.einsum('btgc,gcd->btgd', pooled, w_pool)
    return y.reshape(B, T, D_MODEL) * pool_scale


def diff_attention(q, k, v, q_pos, k_pos, lam):
    s = jnp.einsum('bthmd,bshmd->bhmts', q, k, preferred_element_type=jnp.float32) * (HEAD_DIM ** -0.5)
    mask = (k_pos[None, :] // CHUNK) <= (q_pos[:, None] // CHUNK)
    s = jnp.where(mask, s, jnp.finfo(jnp.float32).min)
    p = jax.nn.softmax(s, axis=-1)
    a = p[:, :, 0] - lam * p[:, :, 1]
    return jnp.einsum('bhts,bshe->bthe', a.astype(v.dtype), v, preferred_element_type=jnp.float32)


def attn_out(o, subln_g, lam_i, dtype):
    B, T = o.shape[0], o.shape[1]
    y = o * lax.rsqrt(jnp.mean(o * o, axis=-1, keepdims=True) + EPS) * subln_g.astype(jnp.float32)
    y = y * (1.0 - lam_i)
    return y.reshape(B, T, ATTN_V).astype(dtype)


def merge(x, pool_y, attn_y, ga, gb, w_out):
    mix = jax.nn.sigmoid(ga) * pool_y + jax.nn.sigmoid(gb) * attn_y
    return x + mix @ w_out


def ffn(x, g, w_in, w_out):
    gu = rms_norm(x, g) @ w_in
    return x + (jax.nn.silu(gu[..., :D_FF]) * gu[..., D_FF:]) @ w_out


def setup_inputs(seed: int = 0) -> dict:
    key = jax.random.key(seed)
    ks = jax.random.split(key, 24)
    f32 = jnp.float32
    nrm = lambda k, shape, scale: jax.random.normal(k, shape, f32) * scale
    gain = lambda k, shape: 1.0 + 0.02 * jax.random.normal(k, shape, f32)
    return {
        'x_prompt': nrm(ks[0], (BATCH, SEQ, D_MODEL), 1.0),
        'x_sample': nrm(ks[1], (DEC_BATCH, DEC_SEQ, D_MODEL), 1.0),
        'cache_pool': nrm(ks[2], (DEPTH, DEC_BATCH, POOL_HIST, POOL_IN), 1.0),
        'cache_k': nrm(ks[3], (DEPTH, DEC_BATCH, PAST_LEN, N_HEADS, 2, HEAD_DIM), 1.0),
        'cache_v': nrm(ks[4], (DEPTH, DEC_BATCH, PAST_LEN, N_HEADS, V_DIM), 1.0),
        'norm_mix': gain(ks[5], (DEPTH, D_MODEL)),
        'w_in': nrm(ks[6], (DEPTH, D_MODEL, IN_COLS), D_MODEL ** -0.5),
        'w_pool': nrm(ks[7], (DEPTH, N_POOL_GROUPS, POOL_GROUP_IN, POOL_GROUP_OUT), POOL_GROUP_IN ** -0.5),
        'pool_scale': gain(ks[8], (DEPTH, D_MODEL)),
        'q_norm': gain(ks[9], (DEPTH, HEAD_DIM)),
        'k_norm': gain(ks[10], (DEPTH, HEAD_DIM)),
        'lambda_q1': nrm(ks[11], (DEPTH, HEAD_DIM), 0.1),
        'lambda_k1': nrm(ks[12], (DEPTH, HEAD_DIM), 0.1),
        'lambda_q2': nrm(ks[13], (DEPTH, HEAD_DIM), 0.1),
        'lambda_k2': nrm(ks[14], (DEPTH, HEAD_DIM), 0.1),
        'subln': gain(ks[15], (DEPTH, V_DIM)),
        'w_out': nrm(ks[16], (DEPTH, D_MODEL, D_MODEL), D_MODEL ** -0.5),
        'norm_ffn': gain(ks[17], (DEPTH, D_MODEL)),
        'w_ffn_in': nrm(ks[18], (DEPTH, D_MODEL, 2 * D_FF), D_MODEL ** -0.5),
        'w_ffn_out': nrm(ks[19], (DEPTH, D_FF, D_MODEL), D_FF ** -0.5),
    }


def reference(x_prompt, x_sample, cache_pool, cache_k, cache_v, norm_mix, w_in, w_pool, pool_scale,
              q_norm, k_norm, lambda_q1, lambda_k1, lambda_q2, lambda_k2, subln, w_out,
              norm_ffn, w_ffn_in, w_ffn_out):
    f32 = jnp.float32
    xp, xs = x_prompt, x_sample
    T_p, T_s = xp.shape[1], xs.shape[1]
    past = cache_k.shape[2]
    pos_p = jnp.arange(T_p)
    pos_s = past + jnp.arange(T_s)
    pos_ks = jnp.arange(past + T_s)
    pool_p, k_p, v_p, pool_s, k_s, v_s = [], [], [], [], [], []
    for l in range(DEPTH):
        lam_i = lambda_init(l)
        lam = (jnp.exp(jnp.sum(lambda_q1[l].astype(f32) * lambda_k1[l].astype(f32)))
               - jnp.exp(jnp.sum(lambda_q2[l].astype(f32) * lambda_k2[l].astype(f32))) + lam_i)

        u, q, k, v, ga, gb = project(xp, norm_mix[l], w_in[l], q_norm[l], k_norm[l])
        u_ext = jnp.concatenate([jnp.zeros((xp.shape[0], POOL_HIST, POOL_IN), u.dtype), u], axis=1)
        pool_y = pool_mix(u_ext, pos_p, w_pool[l], pool_scale[l])
        blocks = []
        for i in range(T_p // QBLOCK):
            e = (i + 1) * QBLOCK
            blocks.append(diff_attention(q[:, e - QBLOCK:e], k[:, :e], v[:, :e],
                                         pos_p[e - QBLOCK:e], pos_p[:e], lam))
        attn_y = attn_out(jnp.concatenate(blocks, axis=1), subln[l], lam_i, xp.dtype)
        xp = merge(xp, pool_y, attn_y, ga, gb, w_out[l])
        xp = ffn(xp, norm_ffn[l], w_ffn_in[l], w_ffn_out[l])
        pool_p.append(u_ext[:, -POOL_HIST:])
        k_p.append(k)
        v_p.append(v)

        u, q, k, v, ga, gb = project(xs, norm_mix[l], w_in[l], q_norm[l], k_norm[l])
        u_ext = jnp.concatenate([cache_pool[l], u], axis=1)
        pool_y = pool_mix(u_ext, pos_s, w_pool[l], pool_scale[l])
        k_all = jnp.concatenate([cache_k[l], k], axis=1)
        v_all = jnp.concatenate([cache_v[l], v], axis=1)
        o = diff_attention(q, k_all, v_all, pos_s, pos_ks, lam)
        attn_y = attn_out(o, subln[l], lam_i, xs.dtype)
        xs = merge(xs, pool_y, attn_y, ga, gb, w_out[l])
        xs = ffn(xs, norm_ffn[l], w_ffn_in[l], w_ffn_out[l])
        pool_s.append(u_ext[:, -POOL_HIST:])
        k_s.append(k)
        v_s.append(v)

    return (xp, xs, jnp.stack(pool_p), jnp.stack(k_p), jnp.stack(v_p),
            jnp.stack(pool_s), jnp.stack(k_s), jnp.stack(v_s))
```

```python
import functools
import math

import jax
import jax.numpy as jnp
from jax import lax
from jax.experimental import pallas as pl
from jax.experimental.pallas import tpu as pltpu

F32 = jnp.float32
BF16 = jnp.bfloat16

CHUNK = 64
POOL_WINDOWS = (2, 4, 8, 16)
POOL_HIST = max(POOL_WINDOWS) - 1
HIST_ROWS = 16
EPS = 1e-6

V7X_VMEM_BYTES = 64 * 1024 * 1024
VMEM_LIMIT = V7X_VMEM_BYTES - 6 * 1024 * 1024


def _params(n_grid_axes):
    return pltpu.CompilerParams(dimension_semantics=("arbitrary",) * n_grid_axes,
                                vmem_limit_bytes=VMEM_LIMIT)


def _tile(n, candidates, also_divides=0):
    for c in candidates:
        if n % c == 0 and also_divides % c == 0:
            return c
    return n


def _lambda_init(layer):
    return 0.8 - 0.6 * math.exp(-0.3 * layer)


def _rmsnorm_kernel(x_ref, g_ref, o_ref):
    x = x_ref[...]
    ms = jnp.mean(x * x, axis=-1, keepdims=True)
    o_ref[...] = (x * lax.rsqrt(ms + EPS) * g_ref[...]).astype(o_ref.dtype)


def _rmsnorm(x, g):
    m, d = x.shape
    tm = _tile(m, (256, 128, 64, 32, 16))
    return pl.pallas_call(
        _rmsnorm_kernel,
        grid=(m // tm,),
        in_specs=[pl.BlockSpec((tm, d), lambda i: (i, 0)),
                  pl.BlockSpec((1, d), lambda i: (0, 0))],
        out_specs=pl.BlockSpec((tm, d), lambda i: (i, 0)),
        out_shape=jax.ShapeDtypeStruct((m, d), BF16),
        compiler_params=_params(1),
        name="rmsnorm",
    )(x, g.reshape(1, d).astype(F32))


def _proj_kernel(a_ref, w_ref, g_ref, *out_refs, mode, scale):
    z = jnp.dot(a_ref[...], w_ref[...], preferred_element_type=F32)
    if mode == "u":
        out_refs[0][...] = z
    elif mode == "v":
        out_refs[0][...] = z
        out_refs[1][...] = z.astype(BF16)
    elif mode == "gate":
        out_refs[0][...] = jax.nn.sigmoid(z).astype(BF16)
    else:
        g = g_ref[...]
        hd = g.shape[-1]
        for c in range(z.shape[1] // hd):
            sl = slice(c * hd, (c + 1) * hd)
            zc = z[:, sl]
            y = zc * lax.rsqrt(jnp.mean(zc * zc, axis=-1, keepdims=True) + EPS) * g
            if mode == "k":
                out_refs[0][:, sl] = y
                out_refs[1][:, sl] = y.astype(BF16)
            else:
                out_refs[0][:, sl] = (y * scale).astype(BF16)


def _proj(h, w, col_off, n, mode, gain=None, scale=1.0):
    m, k = h.shape
    two_out = mode in ("k", "v")
    tm = _tile(m, (1024, 512, 256, 128, 64, 32, 16))
    tn = _tile(n, (512,) if two_out else (1024, 512, 256, 128), also_divides=col_off)
    off = col_off // tn
    if gain is None:
        gain = jnp.ones((1, 128), F32)
    gain = gain.reshape(1, -1).astype(F32)
    o_spec = pl.BlockSpec((tm, tn), lambda i, j: (i, j))
    if mode == "u":
        out_shape, out_specs = [jax.ShapeDtypeStruct((m, n), F32)], [o_spec]
    elif two_out:
        out_shape = [jax.ShapeDtypeStruct((m, n), F32), jax.ShapeDtypeStruct((m, n), BF16)]
        out_specs = [o_spec, o_spec]
    else:
        out_shape, out_specs = [jax.ShapeDtypeStruct((m, n), BF16)], [o_spec]
    outs = pl.pallas_call(
        functools.partial(_proj_kernel, mode=mode, scale=scale),
        grid=(m // tm, n // tn),
        in_specs=[pl.BlockSpec((tm, k), lambda i, j: (i, 0)),
                  pl.BlockSpec((k, tn), lambda i, j: (0, off + j)),
                  pl.BlockSpec(gain.shape, lambda i, j: (0, 0))],
        out_specs=out_specs,
        out_shape=out_shape,
        compiler_params=_params(2),
        name="proj_" + mode,
    )(h, w, gain)
    return outs if two_out else outs[0]


def _lambda(lq1, lk1, lq2, lk2, lam_i):
    a = jnp.sum(lq1[...] * lk1[...], axis=-1, keepdims=True)
    b = jnp.sum(lq2[...] * lk2[...], axis=-1, keepdims=True)
    return jnp.exp(a) - jnp.exp(b) + lam_i


def _head_out(o0, o1, lam, sub, lam_i):
    o = o0 - lam * o1
    y = o * lax.rsqrt(jnp.mean(o * o, axis=-1, keepdims=True) + EPS) * sub
    return y * (1.0 - lam_i)


def _nt_dot(a, b):
    return lax.dot_general(a, b, (((1,), (1,)), ((), ())), preferred_element_type=F32)


def _attn_prompt_kernel(q_ref, k_ref, v_ref, lq1, lk1, lq2, lk2, sub_ref, o_ref,
                        m_sc, l_sc, acc_sc, *, tq, hd, lam_i):
    qi = pl.program_id(2)
    q = q_ref[0]
    q1, q2 = q[:, :hd], q[:, hd:]
    m_sc[...] = jnp.full(m_sc.shape, -jnp.inf, F32)
    l_sc[...] = jnp.zeros(l_sc.shape, F32)
    acc_sc[...] = jnp.zeros(acc_sc.shape, F32)

    def scores(j):
        k = k_ref[0, pl.ds(pl.multiple_of(j * tq, tq), tq), :]
        return jnp.concatenate([_nt_dot(q1, k[:, :hd]), _nt_dot(q2, k[:, hd:])], axis=0)

    def update(j, s):
        v = v_ref[0, pl.ds(pl.multiple_of(j * tq, tq), tq), :]
        m_old = m_sc[...]
        m_new = jnp.maximum(m_old, jnp.max(s, axis=-1, keepdims=True))
        alpha = jnp.exp(m_old - m_new)
        p = jnp.exp(s - m_new)
        l_sc[...] = alpha * l_sc[...] + jnp.sum(p, axis=-1, keepdims=True)
        acc_sc[...] = alpha * acc_sc[...] + jnp.dot(p.astype(BF16), v, preferred_element_type=F32)
        m_sc[...] = m_new

    def body(j, carry):
        update(j, scores(j))
        return carry

    lax.fori_loop(0, qi, body, 0)

    s = scores(qi)
    r = lax.broadcasted_iota(jnp.int32, (tq, tq), 0) // CHUNK
    c = lax.broadcasted_iota(jnp.int32, (tq, tq), 1) // CHUNK
    mask = c <= r
    s = jnp.where(jnp.concatenate([mask, mask], axis=0), s, jnp.finfo(F32).min)
    update(qi, s)

    inv_l = 1.0 / l_sc[...]
    acc = acc_sc[...]
    lam = _lambda(lq1, lk1, lq2, lk2, lam_i)
    y = _head_out(acc[:tq] * inv_l[:tq], acc[tq:] * inv_l[tq:], lam, sub_ref[...], lam_i)
    o_ref[0] = y.astype(o_ref.dtype)


def _attn_prompt(q, k, v, lams, subln, n_heads, lam_i):
    b, t, qcols = q.shape
    hd = qcols // (2 * n_heads)
    dv = v.shape[-1] // n_heads
    tq = _tile(t, (512, 256, 128, 64))
    vec = pl.BlockSpec((1, hd), lambda bi, h, i: (0, 0))
    return pl.pallas_call(
        functools.partial(_attn_prompt_kernel, tq=tq, hd=hd, lam_i=lam_i),
        grid=(b, n_heads, t // tq),
        in_specs=[pl.BlockSpec((1, tq, 2 * hd), lambda bi, h, i: (bi, i, h)),
                  pl.BlockSpec((1, t, 2 * hd), lambda bi, h, i: (bi, 0, h)),
                  pl.BlockSpec((1, t, dv), lambda bi, h, i: (bi, 0, h)),
                  vec, vec, vec, vec,
                  pl.BlockSpec((1, dv), lambda bi, h, i: (0, 0))],
        out_specs=pl.BlockSpec((1, tq, dv), lambda bi, h, i: (bi, i, h)),
        out_shape=jax.ShapeDtypeStruct((b, t, n_heads * dv), BF16),
        scratch_shapes=[pltpu.VMEM((2 * tq, 1), F32), pltpu.VMEM((2 * tq, 1), F32),
                        pltpu.VMEM((2 * tq, dv), F32)],
        compiler_params=_params(3),
        name="attn_prompt",
    )(q, k, v, *lams, subln)


def _attn_cached_kernel(q_ref, kc_ref, kn_ref, vc_ref, vn_ref, lq1, lk1, lq2, lk2, sub_ref, o_ref,
                        *, hd, lam_i):
    q = q_ref[0]
    ts = q.shape[0]
    kc = kc_ref[0].astype(BF16)
    kn = kn_ref[0]
    past = kc.shape[0]
    q_chunk = (past + lax.broadcasted_iota(jnp.int32, (ts, 1), 0)) // CHUNK
    mask_c = lax.broadcasted_iota(jnp.int32, (ts, past), 1) // CHUNK <= q_chunk
    mask_n = (past + lax.broadcasted_iota(jnp.int32, (ts, ts), 1)) // CHUNK <= q_chunk
    neg = jnp.finfo(F32).min
    outs = []
    for half in range(2):
        sl = slice(half * hd, (half + 1) * hd)
        sc = jnp.where(mask_c, _nt_dot(q[:, sl], kc[:, sl]), neg)
        sn = jnp.where(mask_n, _nt_dot(q[:, sl], kn[:, sl]), neg)
        m = jnp.maximum(jnp.max(sc, axis=-1, keepdims=True), jnp.max(sn, axis=-1, keepdims=True))
        pc = jnp.exp(sc - m)
        pn = jnp.exp(sn - m)
        l = jnp.sum(pc, axis=-1, keepdims=True) + jnp.sum(pn, axis=-1, keepdims=True)
        o = (jnp.dot(pc.astype(BF16), vc_ref[0].astype(BF16), preferred_element_type=F32)
             + jnp.dot(pn.astype(BF16), vn_ref[0], preferred_element_type=F32))
        outs.append(o * (1.0 / l))
    lam = _lambda(lq1, lk1, lq2, lk2, lam_i)
    o_ref[0] = _head_out(outs[0], outs[1], lam, sub_ref[...], lam_i).astype(o_ref.dtype)


def _attn_cached(q, k_cache, k_new, v_cache, v_new, lams, subln, n_heads, lam_i):
    b, ts, qcols = q.shape
    past = k_cache.shape[1]
    hd = qcols // (2 * n_heads)
    dv = v_new.shape[-1] // n_heads
    vec = pl.BlockSpec((1, hd), lambda bi, h: (0, 0))
    return pl.pallas_call(
        functools.partial(_attn_cached_kernel, hd=hd, lam_i=lam_i),
        grid=(b, n_heads),
        in_specs=[pl.BlockSpec((1, ts, 2 * hd), lambda bi, h: (bi, 0, h)),
                  pl.BlockSpec((1, past, 2 * hd), lambda bi, h: (bi, 0, h)),
                  pl.BlockSpec((1, ts, 2 * hd), lambda bi, h: (bi, 0, h)),
                  pl.BlockSpec((1, past, dv), lambda bi, h: (bi, 0, h)),
                  pl.BlockSpec((1, ts, dv), lambda bi, h: (bi, 0, h)),
                  vec, vec, vec, vec,
                  pl.BlockSpec((1, dv), lambda bi, h: (0, 0))],
        out_specs=pl.BlockSpec((1, ts, dv), lambda bi, h: (bi, 0, h)),
        out_shape=jax.ShapeDtypeStruct((b, ts, n_heads * dv), BF16),
        compiler_params=_params(2),
        name="attn_cached",
    )(q, k_cache, k_new, v_cache, v_new, *lams, subln)


def _poolmix_kernel(u_ref, uprev_ref, hist0_ref, wp_ref, ps_ref, ga_ref, gb_ref, at_ref, o_ref, ext_sc,
                    *, tm, pos0):
    i = pl.program_id(1)
    hist = jnp.where(i == 0, hist0_ref[0], uprev_ref[0])
    ext_sc[0:HIST_ROWS, :] = hist
    ext_sc[HIST_ROWS:HIST_ROWS + tm, :] = u_ref[0]
    pos = pos0 + i * tm + lax.broadcasted_iota(jnp.int32, (tm, 1), 0)
    gin = wp_ref.shape[1]
    gout = wp_ref.shape[2]
    for g, w in enumerate(POOL_WINDOWS):
        cin = slice(g * gin, (g + 1) * gin)
        cout = slice(g * gout, (g + 1) * gout)
        u_new = ext_sc[HIST_ROWS:HIST_ROWS + tm, cin]
        win = u_new
        for j in range(1, w):
            win = win + ext_sc[HIST_ROWS - j:HIST_ROWS - j + tm, cin]
        inv_cnt = 1.0 / jnp.minimum(pos + 1, w).astype(F32)
        pooled = win * inv_cnt - u_new
        y = jnp.dot(pooled.astype(BF16), wp_ref[g], preferred_element_type=F32) * ps_ref[:, cout]
        mix = ga_ref[0, :, cout].astype(F32) * y + gb_ref[0, :, cout].astype(F32) * at_ref[0, :, cout].astype(F32)
        o_ref[0, :, cout] = mix.astype(o_ref.dtype)


def _poolmix(u, hist0, w_pool, pool_scale, gates, attn, pos0):
    b, t, pin = u.shape
    d = attn.shape[-1]
    tm = _tile(t, (256, 128, 64, 32, 16))
    hb = tm // HIST_ROWS
    return pl.pallas_call(
        functools.partial(_poolmix_kernel, tm=tm, pos0=pos0),
        grid=(b, t // tm),
        in_specs=[pl.BlockSpec((1, tm, pin), lambda bi, i: (bi, i, 0)),
                  pl.BlockSpec((1, HIST_ROWS, pin), lambda bi, i: (bi, jnp.maximum(i * hb - 1, 0), 0)),
                  pl.BlockSpec((1, HIST_ROWS, pin), lambda bi, i: (bi, 0, 0)),
                  pl.BlockSpec(w_pool.shape, lambda bi, i: (0, 0, 0)),
                  pl.BlockSpec((1, d), lambda bi, i: (0, 0)),
                  pl.BlockSpec((1, tm, d), lambda bi, i: (bi, i, 0)),
                  pl.BlockSpec((1, tm, d), lambda bi, i: (bi, i, 1)),
                  pl.BlockSpec((1, tm, d), lambda bi, i: (bi, i, 0))],
        out_specs=pl.BlockSpec((1, tm, d), lambda bi, i: (bi, i, 0)),
        out_shape=jax.ShapeDtypeStruct((b, t, d), BF16),
        scratch_shapes=[pltpu.VMEM((HIST_ROWS + tm, pin), F32)],
        compiler_params=_params(2),
        name="poolmix",
    )(u, u, hist0, w_pool, pool_scale.reshape(1, d).astype(F32), gates, gates, attn)


def _mm_res_kernel(a_ref, w_ref, r_ref, o_ref):
    o_ref[...] = r_ref[...] + jnp.dot(a_ref[...], w_ref[...], preferred_element_type=F32)


def _mm_res(a, w, res):
    m, k = a.shape
    n = w.shape[1]
    big_k = k > 8192
    tm = _tile(m, (512,) if big_k else (1024, 512, 256, 128, 64, 32, 16))
    tn = _tile(n, (256,) if big_k else (1024, 512, 256, 128))
    return pl.pallas_call(
        _mm_res_kernel,
        grid=(m // tm, n // tn),
        in_specs=[pl.BlockSpec((tm, k), lambda i, j: (i, 0)),
                  pl.BlockSpec((k, tn), lambda i, j: (0, j)),
                  pl.BlockSpec((tm, tn), lambda i, j: (i, j))],
        out_specs=pl.BlockSpec((tm, tn), lambda i, j: (i, j)),
        out_shape=jax.ShapeDtypeStruct((m, n), F32),
        compiler_params=_params(2),
        name="mm_res",
    )(a, w, res)


def _swiglu_kernel(a_ref, wg_ref, wu_ref, o_ref):
    a = a_ref[...]
    gate = jnp.dot(a, wg_ref[...], preferred_element_type=F32)
    up = jnp.dot(a, wu_ref[...], preferred_element_type=F32)
    o_ref[...] = (gate * jax.nn.sigmoid(gate) * up).astype(o_ref.dtype)


def _swiglu(h, w, d_ff):
    m, k = h.shape
    tm = _tile(m, (1024, 512, 256, 128, 64, 32, 16))
    tn = _tile(d_ff, (256, 128))
    nb = d_ff // tn
    return pl.pallas_call(
        _swiglu_kernel,
        grid=(m // tm, nb),
        in_specs=[pl.BlockSpec((tm, k), lambda i, j: (i, 0)),
                  pl.BlockSpec((k, tn), lambda i, j: (0, j)),
                  pl.BlockSpec((k, tn), lambda i, j: (0, nb + j))],
        out_specs=pl.BlockSpec((tm, tn), lambda i, j: (i, j)),
        out_shape=jax.ShapeDtypeStruct((m, d_ff), BF16),
        compiler_params=_params(2),
        name="swiglu",
    )(h, w, w)


def _layer(x, cache, layer, p):
    b, t, d = x.shape
    n_heads, hd, dv = p["n_heads"], p["hd"], p["dv"]
    pool_in = p["w_pool"].shape[0] * p["w_pool"].shape[1]
    attn_qk = n_heads * 2 * hd
    attn_v = n_heads * dv
    o1 = pool_in
    o2 = o1 + attn_qk
    o3 = o2 + attn_qk
    o4 = o3 + attn_v
    lam_i = _lambda_init(layer)
    x2 = x.reshape(b * t, d)

    h = _rmsnorm(x2, p["norm_mix"])
    u = _proj(h, p["w_in"], 0, pool_in, "u")
    q = _proj(h, p["w_in"], o1, attn_qk, "q", gain=p["q_norm"], scale=hd ** -0.5)
    k32, k16 = _proj(h, p["w_in"], o2, attn_qk, "k", gain=p["k_norm"])
    v32, v16 = _proj(h, p["w_in"], o3, attn_v, "v")
    gates = _proj(h, p["w_in"], o4, 2 * d, "gate")

    u3 = u.reshape(b, t, pool_in)
    q3, k3, v3 = q.reshape(b, t, attn_qk), k16.reshape(b, t, attn_qk), v16.reshape(b, t, attn_v)
    if cache is None:
        pos0 = 0
        hist = jnp.zeros((b, POOL_HIST, pool_in), F32)
        attn = _attn_prompt(q3, k3, v3, p["lams"], p["subln"], n_heads, lam_i)
    else:
        cache_pool, cache_k, cache_v = cache
        pos0 = cache_k.shape[1]
        hist = cache_pool
        attn = _attn_cached(q3, cache_k.reshape(b, pos0, attn_qk), k3, cache_v.reshape(b, pos0, attn_v), v3,
                            p["lams"], p["subln"], n_heads, lam_i)
    hist0 = jnp.pad(hist, ((0, 0), (HIST_ROWS - POOL_HIST, 0), (0, 0)))
    mix = _poolmix(u3, hist0, p["w_pool"], p["pool_scale"], gates.reshape(b, t, 2 * d), attn, pos0)

    x1 = _mm_res(mix.reshape(b * t, d), p["w_out"], x2)
    h2 = _rmsnorm(x1, p["norm_ffn"])
    act = _swiglu(h2, p["w_ffn_in"], p["d_ff"])
    y = _mm_res(act, p["w_ffn_out"], x1)

    new_pool = jnp.concatenate([hist, u3], axis=1)[:, -POOL_HIST:] if t < POOL_HIST else u3[:, t - POOL_HIST:]
    new_k = k32.reshape(b, t, n_heads, 2, hd)
    new_v = v32.reshape(b, t, n_heads, dv)
    return y.reshape(b, t, d), new_pool, new_k, new_v


def kernel(x_prompt, x_sample, cache_pool, cache_k, cache_v, norm_mix, w_in, w_pool, pool_scale, q_norm, k_norm,
           lambda_q1, lambda_k1, lambda_q2, lambda_k2, subln, w_out, norm_ffn, w_ffn_in, w_ffn_out):
    depth = w_in.shape[0]
    n_heads, hd, dv = cache_k.shape[3], cache_k.shape[5], cache_v.shape[4]
    xp, xs = x_prompt, x_sample
    outs_p, outs_s = [], []
    for l in range(depth):
        vec = lambda a: a[l].reshape(1, -1).astype(F32)
        p = dict(
            n_heads=n_heads, hd=hd, dv=dv, d_ff=w_ffn_out.shape[1],
            norm_mix=norm_mix[l], norm_ffn=norm_ffn[l], q_norm=q_norm[l], k_norm=k_norm[l],
            pool_scale=pool_scale[l], subln=vec(subln),
            lams=(vec(lambda_q1), vec(lambda_k1), vec(lambda_q2), vec(lambda_k2)),
            w_in=w_in[l].astype(BF16), w_pool=w_pool[l].astype(BF16), w_out=w_out[l].astype(BF16),
            w_ffn_in=w_ffn_in[l].astype(BF16), w_ffn_out=w_ffn_out[l].astype(BF16),
        )
        xp, pool_p, k_p, v_p = _layer(xp, None, l, p)
        xs, pool_s, k_s, v_s = _layer(xs, (cache_pool[l], cache_k[l], cache_v[l]), l, p)
        outs_p.append((pool_p, k_p, v_p))
        outs_s.append((pool_s, k_s, v_s))
    stack = lambda outs, i: outs[0][i][None] if depth == 1 else jnp.stack([o[i] for o in outs])
    return (xp, xs, stack(outs_p, 0), stack(outs_p, 1), stack(outs_p, 2),
            stack(outs_s, 0), stack(outs_s, 1), stack(outs_s, 2))
```

```python
import functools
import math

import jax
import jax.numpy as jnp
from jax import lax
from jax.experimental import pallas as pl
from jax.experimental.pallas import tpu as pltpu

F32 = jnp.float32
BF16 = jnp.bfloat16

CHUNK = 64
POOL_WINDOWS = (2, 4, 8, 16)
POOL_HIST = max(POOL_WINDOWS) - 1
HIST_ROWS = 16
EPS = 1e-6

V7X_VMEM_BYTES = 64 * 1024 * 1024
VMEM_LIMIT = V7X_VMEM_BYTES - 6 * 1024 * 1024


def _params(n_grid_axes):
    return pltpu.CompilerParams(dimension_semantics=("arbitrary",) * n_grid_axes,
                                vmem_limit_bytes=VMEM_LIMIT)


def _tile(n, candidates, also_divides=0):
    for c in candidates:
        if n % c == 0 and also_divides % c == 0:
            return c
    return n


def _lambda_init(layer):
    return 0.8 - 0.6 * math.exp(-0.3 * layer)


def _rmsnorm_kernel(x_ref, g_ref, o_ref):
    x = x_ref[...]
    ms = jnp.mean(x * x, axis=-1, keepdims=True)
    o_ref[...] = (x * lax.rsqrt(ms + EPS) * g_ref[...]).astype(o_ref.dtype)


def _rmsnorm(x, g):
    m, d = x.shape
    tm = _tile(m, (256, 128, 64, 32, 16))
    return pl.pallas_call(
        _rmsnorm_kernel,
        grid=(m // tm,),
        in_specs=[pl.BlockSpec((tm, d), lambda i: (i, 0)),
                  pl.BlockSpec((1, d), lambda i: (0, 0))],
        out_specs=pl.BlockSpec((tm, d), lambda i: (i, 0)),
        out_shape=jax.ShapeDtypeStruct((m, d), BF16),
        compiler_params=_params(1),
        name="rmsnorm",
    )(x, g.reshape(1, d).astype(F32))


def _proj_kernel(a_ref, w_ref, g_ref, *out_refs, mode, scale, transposed):
    z = jnp.dot(a_ref[...], w_ref[...], preferred_element_type=F32)
    if mode == "u":
        out_refs[0][...] = z
    elif mode == "v":
        out_refs[0][...] = z
        if transposed:
            out_refs[1][0] = z.T.astype(BF16)
        else:
            out_refs[1][...] = z.astype(BF16)
    elif mode == "gate":
        out_refs[0][...] = jax.nn.sigmoid(z).astype(BF16)
    else:
        g = g_ref[...]
        hd = g.shape[-1]
        for c in range(z.shape[1] // hd):
            sl = slice(c * hd, (c + 1) * hd)
            zc = z[:, sl]
            y = zc * lax.rsqrt(jnp.mean(zc * zc, axis=-1, keepdims=True) + EPS) * g
            if mode == "k":
                out_refs[0][:, sl] = y
                out_refs[1][:, sl] = y.astype(BF16)
            elif transposed:
                out_refs[0][sl, :] = (y * scale).T.astype(BF16)
            else:
                out_refs[0][:, sl] = (y * scale).astype(BF16)


def _proj(h, w, col_off, n, mode, gain=None, scale=1.0, transposed_tile=0):
    m, k = h.shape
    two_out = mode in ("k", "v")
    transposed = transposed_tile > 0
    if transposed and mode == "v":
        tm = transposed_tile
    else:
        tm = _tile(m, (1024, 512, 256, 128, 64, 32, 16))
    tn = _tile(n, (512,) if two_out else (1024, 512, 256, 128), also_divides=col_off)
    off = col_off // tn
    if gain is None:
        gain = jnp.ones((1, 128), F32)
    gain = gain.reshape(1, -1).astype(F32)
    o_spec = pl.BlockSpec((tm, tn), lambda i, j: (i, j))
    if mode == "u":
        out_shape, out_specs = [jax.ShapeDtypeStruct((m, n), F32)], [o_spec]
    elif mode == "v" and transposed:
        out_shape = [jax.ShapeDtypeStruct((m, n), F32), jax.ShapeDtypeStruct((m // tm, n, tm), BF16)]
        out_specs = [o_spec, pl.BlockSpec((1, tn, tm), lambda i, j: (i, j, 0))]
    elif two_out:
        out_shape = [jax.ShapeDtypeStruct((m, n), F32), jax.ShapeDtypeStruct((m, n), BF16)]
        out_specs = [o_spec, o_spec]
    elif transposed:
        out_shape = [jax.ShapeDtypeStruct((n, m), BF16)]
        out_specs = [pl.BlockSpec((tn, tm), lambda i, j: (j, i))]
    else:
        out_shape, out_specs = [jax.ShapeDtypeStruct((m, n), BF16)], [o_spec]
    outs = pl.pallas_call(
        functools.partial(_proj_kernel, mode=mode, scale=scale, transposed=transposed),
        grid=(m // tm, n // tn),
        in_specs=[pl.BlockSpec((tm, k), lambda i, j: (i, 0)),
                  pl.BlockSpec((k, tn), lambda i, j: (0, off + j)),
                  pl.BlockSpec(gain.shape, lambda i, j: (0, 0))],
        out_specs=out_specs,
        out_shape=out_shape,
        compiler_params=_params(2),
        name="proj_" + mode,
    )(h, w, gain)
    return outs if two_out else outs[0]


def _lambda(lq1, lk1, lq2, lk2, lam_i):
    a = jnp.sum(lq1[...] * lk1[...], axis=-1, keepdims=True)
    b = jnp.sum(lq2[...] * lk2[...], axis=-1, keepdims=True)
    return jnp.exp(a) - jnp.exp(b) + lam_i


def _head_out(o0, o1, lam, sub, lam_i):
    o = o0 - lam * o1
    y = o * lax.rsqrt(jnp.mean(o * o, axis=-1, keepdims=True) + EPS) * sub
    return y * (1.0 - lam_i)


def _nt_dot(a, b):
    return lax.dot_general(a, b, (((1,), (1,)), ((), ())), preferred_element_type=F32)


def _attn_prompt_kernel(qt_ref, k_ref, vt_ref, lq1, lk1, lq2, lk2, sub_ref, o_ref, acc_sc,
                        *, tq, tk, hd, lam_i):
    qi = pl.program_id(2)
    q1t = qt_ref[:hd, :]
    q2t = qt_ref[hd:, :]
    acc_sc[...] = jnp.zeros(acc_sc.shape, F32)

    def scores(j):
        k = k_ref[0, pl.ds(pl.multiple_of(j * tk, tk), tk), :]
        return jnp.concatenate([jnp.dot(k[:, :hd], q1t, preferred_element_type=F32),
                                jnp.dot(k[:, hd:], q2t, preferred_element_type=F32)], axis=1)

    def update(j, s, m_old, l_old):
        m_new = jnp.maximum(m_old, jnp.max(s, axis=0, keepdims=True))
        alpha = jnp.exp(m_old - m_new)
        p = jnp.exp(s - m_new)
        l_new = alpha * l_old + jnp.sum(p, axis=0, keepdims=True)
        acc_sc[...] = alpha * acc_sc[...] + jnp.dot(vt_ref[j], p.astype(BF16), preferred_element_type=F32)
        return m_new, l_new

    def body(j, carry):
        return update(j, scores(j), *carry)

    n_full = qi * (tq // tk)
    init = (jnp.full((1, 2 * tq), -jnp.inf, F32), jnp.zeros((1, 2 * tq), F32))
    m, l = lax.fori_loop(0, n_full, body, init)

    q_chunk = lax.broadcasted_iota(jnp.int32, (tk, tq), 1) // CHUNK
    for d in range(tq // tk):
        k_chunk = (d * tk + lax.broadcasted_iota(jnp.int32, (tk, tq), 0)) // CHUNK
        mask = k_chunk <= q_chunk
        s = jnp.where(jnp.concatenate([mask, mask], axis=1), scores(n_full + d), jnp.finfo(F32).min)
        m, l = update(n_full + d, s, m, l)

    inv_l = 1.0 / l
    acc = acc_sc[...]
    lam = _lambda(lq1, lk1, lq2, lk2, lam_i)
    o0 = (acc[:, :tq] * inv_l[:, :tq]).T
    o1 = (acc[:, tq:] * inv_l[:, tq:]).T
    o_ref[0] = _head_out(o0, o1, lam, sub_ref[...], lam_i).astype(o_ref.dtype)


def _attn_prompt(qt, k, vt, lams, subln, n_heads, lam_i, tq, tk):
    b, t, qcols = k.shape
    hd = qcols // (2 * n_heads)
    dv = vt.shape[1] // n_heads
    nq = t // tq
    vec = pl.BlockSpec((1, hd), lambda bi, h, i: (0, 0))
    return pl.pallas_call(
        functools.partial(_attn_prompt_kernel, tq=tq, tk=tk, hd=hd, lam_i=lam_i),
        grid=(b, n_heads, nq),
        in_specs=[pl.BlockSpec((2 * hd, tq), lambda bi, h, i: (h, bi * nq + i)),
                  pl.BlockSpec((1, t, 2 * hd), lambda bi, h, i: (bi, 0, h)),
                  pl.BlockSpec((t // tk, dv, tk), lambda bi, h, i: (bi, h, 0)),
                  vec, vec, vec, vec,
                  pl.BlockSpec((1, dv), lambda bi, h, i: (0, 0))],
        out_specs=pl.BlockSpec((1, tq, dv), lambda bi, h, i: (bi, i, h)),
        out_shape=jax.ShapeDtypeStruct((b, t, n_heads * dv), BF16),
        scratch_shapes=[pltpu.VMEM((dv, 2 * tq), F32)],
        compiler_params=_params(3),
        name="attn_prompt",
    )(qt, k, vt, *lams, subln)


def _attn_cached_kernel(q_ref, kc_ref, kn_ref, vc_ref, vn_ref, lq1, lk1, lq2, lk2, sub_ref, o_ref,
                        *, hd, lam_i):
    q = q_ref[0]
    ts = q.shape[0]
    kc = kc_ref[0].astype(BF16)
    kn = kn_ref[0]
    past = kc.shape[0]
    q_chunk = (past + lax.broadcasted_iota(jnp.int32, (ts, 1), 0)) // CHUNK
    mask_c = lax.broadcasted_iota(jnp.int32, (ts, past), 1) // CHUNK <= q_chunk
    mask_n = (past + lax.broadcasted_iota(jnp.int32, (ts, ts), 1)) // CHUNK <= q_chunk
    neg = jnp.finfo(F32).min
    outs = []
    for half in range(2):
        sl = slice(half * hd, (half + 1) * hd)
        sc = jnp.where(mask_c, _nt_dot(q[:, sl], kc[:, sl]), neg)
        sn = jnp.where(mask_n, _nt_dot(q[:, sl], kn[:, sl]), neg)
        m = jnp.maximum(jnp.max(sc, axis=-1, keepdims=True), jnp.max(sn, axis=-1, keepdims=True))
        pc = jnp.exp(sc - m)
        pn = jnp.exp(sn - m)
        l = jnp.sum(pc, axis=-1, keepdims=True) + jnp.sum(pn, axis=-1, keepdims=True)
        o = (jnp.dot(pc.astype(BF16), vc_ref[0].astype(BF16), preferred_element_type=F32)
             + jnp.dot(pn.astype(BF16), vn_ref[0], preferred_element_type=F32))
        outs.append(o * (1.0 / l))
    lam = _lambda(lq1, lk1, lq2, lk2, lam_i)
    o_ref[0] = _head_out(outs[0], outs[1], lam, sub_ref[...], lam_i).astype(o_ref.dtype)


def _attn_cached(q, k_cache, k_new, v_cache, v_new, lams, subln, n_heads, lam_i):
    b, ts, qcols = q.shape
    past = k_cache.shape[1]
    hd = qcols // (2 * n_heads)
    dv = v_new.shape[-1] // n_heads
    vec = pl.BlockSpec((1, hd), lambda bi, h: (0, 0))
    return pl.pallas_call(
        functools.partial(_attn_cached_kernel, hd=hd, lam_i=lam_i),
        grid=(b, n_heads),
        in_specs=[pl.BlockSpec((1, ts, 2 * hd), lambda bi, h: (bi, 0, h)),
                  pl.BlockSpec((1, past, 2 * hd), lambda bi, h: (bi, 0, h)),
                  pl.BlockSpec((1, ts, 2 * hd), lambda bi, h: (bi, 0, h)),
                  pl.BlockSpec((1, past, dv), lambda bi, h: (bi, 0, h)),
                  pl.BlockSpec((1, ts, dv), lambda bi, h: (bi, 0, h)),
                  vec, vec, vec, vec,
                  pl.BlockSpec((1, dv), lambda bi, h: (0, 0))],
        out_specs=pl.BlockSpec((1, ts, dv), lambda bi, h: (bi, 0, h)),
        out_shape=jax.ShapeDtypeStruct((b, ts, n_heads * dv), BF16),
        compiler_params=_params(2),
        name="attn_cached",
    )(q, k_cache, k_new, v_cache, v_new, *lams, subln)


def _poolmix_kernel(u_ref, uprev_ref, hist0_ref, wp_ref, ps_ref, ga_ref, gb_ref, at_ref, o_ref, ext_sc,
                    *, tm, pos0):
    i = pl.program_id(1)
    hist = jnp.where(i == 0, hist0_ref[0], uprev_ref[0])
    ext_sc[0:HIST_ROWS, :] = hist
    ext_sc[HIST_ROWS:HIST_ROWS + tm, :] = u_ref[0]
    pos = pos0 + i * tm + lax.broadcasted_iota(jnp.int32, (tm, 1), 0)
    gin = wp_ref.shape[1]
    gout = wp_ref.shape[2]
    for g, w in enumerate(POOL_WINDOWS):
        cin = slice(g * gin, (g + 1) * gin)
        cout = slice(g * gout, (g + 1) * gout)
        u_new = ext_sc[HIST_ROWS:HIST_ROWS + tm, cin]
        win = u_new
        for j in range(1, w):
            win = win + ext_sc[HIST_ROWS - j:HIST_ROWS - j + tm, cin]
        inv_cnt = 1.0 / jnp.minimum(pos + 1, w).astype(F32)
        pooled = win * inv_cnt - u_new
        y = jnp.dot(pooled.astype(BF16), wp_ref[g], preferred_element_type=F32) * ps_ref[:, cout]
        mix = ga_ref[0, :, cout].astype(F32) * y + gb_ref[0, :, cout].astype(F32) * at_ref[0, :, cout].astype(F32)
        o_ref[0, :, cout] = mix.astype(o_ref.dtype)


def _poolmix(u, hist0, w_pool, pool_scale, gates, attn, pos0):
    b, t, pin = u.shape
    d = attn.shape[-1]
    tm = _tile(t, (256, 128, 64, 32, 16))
    hb = tm // HIST_ROWS
    return pl.pallas_call(
        functools.partial(_poolmix_kernel, tm=tm, pos0=pos0),
        grid=(b, t // tm),
        in_specs=[pl.BlockSpec((1, tm, pin), lambda bi, i: (bi, i, 0)),
                  pl.BlockSpec((1, HIST_ROWS, pin), lambda bi, i: (bi, jnp.maximum(i * hb - 1, 0), 0)),
                  pl.BlockSpec((1, HIST_ROWS, pin), lambda bi, i: (bi, 0, 0)),
                  pl.BlockSpec(w_pool.shape, lambda bi, i: (0, 0, 0)),
                  pl.BlockSpec((1, d), lambda bi, i: (0, 0)),
                  pl.BlockSpec((1, tm, d), lambda bi, i: (bi, i, 0)),
                  pl.BlockSpec((1, tm, d), lambda bi, i: (bi, i, 1)),
                  pl.BlockSpec((1, tm, d), lambda bi, i: (bi, i, 0))],
        out_specs=pl.BlockSpec((1, tm, d), lambda bi, i: (bi, i, 0)),
        out_shape=jax.ShapeDtypeStruct((b, t, d), BF16),
        scratch_shapes=[pltpu.VMEM((HIST_ROWS + tm, pin), F32)],
        compiler_params=_params(2),
        name="poolmix",
    )(u, u, hist0, w_pool, pool_scale.reshape(1, d).astype(F32), gates, gates, attn)


def _mm_res_kernel(a_ref, w_ref, r_ref, o_ref):
    o_ref[...] = r_ref[...] + jnp.dot(a_ref[...], w_ref[...], preferred_element_type=F32)


def _mm_res(a, w, res):
    m, k = a.shape
    n = w.shape[1]
    big_k = k > 8192
    tm = _tile(m, (512,) if big_k else (1024, 512, 256, 128, 64, 32, 16))
    tn = _tile(n, (256,) if big_k else (1024, 512, 256, 128))
    return pl.pallas_call(
        _mm_res_kernel,
        grid=(m // tm, n // tn),
        in_specs=[pl.BlockSpec((tm, k), lambda i, j: (i, 0)),
                  pl.BlockSpec((k, tn), lambda i, j: (0, j)),
                  pl.BlockSpec((tm, tn), lambda i, j: (i, j))],
        out_specs=pl.BlockSpec((tm, tn), lambda i, j: (i, j)),
        out_shape=jax.ShapeDtypeStruct((m, n), F32),
        compiler_params=_params(2),
        name="mm_res",
    )(a, w, res)


def _swiglu_kernel(a_ref, wg_ref, wu_ref, o_ref):
    a = a_ref[...]
    gate = jnp.dot(a, wg_ref[...], preferred_element_type=F32)
    up = jnp.dot(a, wu_ref[...], preferred_element_type=F32)
    o_ref[...] = (gate * jax.nn.sigmoid(gate) * up).astype(o_ref.dtype)


def _swiglu(h, w, d_ff):
    m, k = h.shape
    tm = _tile(m, (1024, 512, 256, 128, 64, 32, 16))
    tn = _tile(d_ff, (256, 128))
    nb = d_ff // tn
    return pl.pallas_call(
        _swiglu_kernel,
        grid=(m // tm, nb),
        in_specs=[pl.BlockSpec((tm, k), lambda i, j: (i, 0)),
                  pl.BlockSpec((k, tn), lambda i, j: (0, j)),
                  pl.BlockSpec((k, tn), lambda i, j: (0, nb + j))],
        out_specs=pl.BlockSpec((tm, tn), lambda i, j: (i, j)),
        out_shape=jax.ShapeDtypeStruct((m, d_ff), BF16),
        compiler_params=_params(2),
        name="swiglu",
    )(h, w, w)


def _layer(x, cache, layer, p):
    b, t, d = x.shape
    n_heads, hd, dv = p["n_heads"], p["hd"], p["dv"]
    pool_in = p["w_pool"].shape[0] * p["w_pool"].shape[1]
    attn_qk = n_heads * 2 * hd
    attn_v = n_heads * dv
    o1 = pool_in
    o2 = o1 + attn_qk
    o3 = o2 + attn_qk
    o4 = o3 + attn_v
    lam_i = _lambda_init(layer)
    x2 = x.reshape(b * t, d)

    h = _rmsnorm(x2, p["norm_mix"])
    u = _proj(h, p["w_in"], 0, pool_in, "u")
    k32, k16 = _proj(h, p["w_in"], o2, attn_qk, "k", gain=p["k_norm"])
    gates = _proj(h, p["w_in"], o4, 2 * d, "gate")
    u3 = u.reshape(b, t, pool_in)
    k3 = k16.reshape(b, t, attn_qk)
    if cache is None:
        pos0 = 0
        hist = jnp.zeros((b, POOL_HIST, pool_in), F32)
        tq = _tile(t, (512, 256, 128, 64))
        tk = _tile(tq, (512, 256, 128))
        qt = _proj(h, p["w_in"], o1, attn_qk, "q", gain=p["q_norm"], scale=hd ** -0.5, transposed_tile=tq)
        v32, vt = _proj(h, p["w_in"], o3, attn_v, "v", transposed_tile=tk)
        attn = _attn_prompt(qt, k3, vt, p["lams"], p["subln"], n_heads, lam_i, tq, tk)
    else:
        cache_pool, cache_k, cache_v = cache
        pos0 = cache_k.shape[1]
        hist = cache_pool
        q = _proj(h, p["w_in"], o1, attn_qk, "q", gain=p["q_norm"], scale=hd ** -0.5)
        v32, v16 = _proj(h, p["w_in"], o3, attn_v, "v")
        attn = _attn_cached(q.reshape(b, t, attn_qk), cache_k.reshape(b, pos0, attn_qk), k3,
                            cache_v.reshape(b, pos0, attn_v), v16.reshape(b, t, attn_v),
                            p["lams"], p["subln"], n_heads, lam_i)
    hist0 = jnp.pad(hist, ((0, 0), (HIST_ROWS - POOL_HIST, 0), (0, 0)))
    mix = _poolmix(u3, hist0, p["w_pool"], p["pool_scale"], gates.reshape(b, t, 2 * d), attn, pos0)

    x1 = _mm_res(mix.reshape(b * t, d), p["w_out"], x2)
    h2 = _rmsnorm(x1, p["norm_ffn"])
    act = _swiglu(h2, p["w_ffn_in"], p["d_ff"])
    y = _mm_res(act, p["w_ffn_out"], x1)

    new_pool = jnp.concatenate([hist, u3], axis=1)[:, -POOL_HIST:] if t < POOL_HIST else u3[:, t - POOL_HIST:]
    new_k = k32.reshape(b, t, n_heads, 2, hd)
    new_v = v32.reshape(b, t, n_heads, dv)
    return y.reshape(b, t, d), new_pool, new_k, new_v


def kernel(x_prompt, x_sample, cache_pool, cache_k, cache_v, norm_mix, w_in, w_pool, pool_scale, q_norm, k_norm,
           lambda_q1, lambda_k1, lambda_q2, lambda_k2, subln, w_out, norm_ffn, w_ffn_in, w_ffn_out):
    depth = w_in.shape[0]
    n_heads, hd, dv = cache_k.shape[3], cache_k.shape[5], cache_v.shape[4]
    xp, xs = x_prompt, x_sample
    outs_p, outs_s = [], []
    for l in range(depth):
        vec = lambda a: a[l].reshape(1, -1).astype(F32)
        p = dict(
            n_heads=n_heads, hd=hd, dv=dv, d_ff=w_ffn_out.shape[1],
            norm_mix=norm_mix[l], norm_ffn=norm_ffn[l], q_norm=q_norm[l], k_norm=k_norm[l],
            pool_scale=pool_scale[l], subln=vec(subln),
            lams=(vec(lambda_q1), vec(lambda_k1), vec(lambda_q2), vec(lambda_k2)),
            w_in=w_in[l].astype(BF16), w_pool=w_pool[l].astype(BF16), w_out=w_out[l].astype(BF16),
            w_ffn_in=w_ffn_in[l].astype(BF16), w_ffn_out=w_ffn_out[l].astype(BF16),
        )
        xp, pool_p, k_p, v_p = _layer(xp, None, l, p)
        xs, pool_s, k_s, v_s = _layer(xs, (cache_pool[l], cache_k[l], cache_v[l]), l, p)
        outs_p.append((pool_p, k_p, v_p))
        outs_s.append((pool_s, k_s, v_s))
    stack = lambda outs, i: outs[0][i][None] if depth == 1 else jnp.stack([o[i] for o in outs])
    return (xp, xs, stack(outs_p, 0), stack(outs_p, 1), stack(outs_p, 2),
            stack(outs_s, 0), stack(outs_s, 1), stack(outs_s, 2))
```

```python
import functools
import math

import jax
import jax.numpy as jnp
from jax import lax
from jax.experimental import pallas as pl
from jax.experimental.pallas import tpu as pltpu

F32 = jnp.float32
BF16 = jnp.bfloat16

CHUNK = 64
POOL_WINDOWS = (2, 4, 8, 16)
POOL_HIST = max(POOL_WINDOWS) - 1
HIST_ROWS = 16
EPS = 1e-6

V7X_VMEM_BYTES = 64 * 1024 * 1024
VMEM_LIMIT = V7X_VMEM_BYTES - 6 * 1024 * 1024


def _params(n_grid_axes):
    return pltpu.CompilerParams(dimension_semantics=("arbitrary",) * n_grid_axes,
                                vmem_limit_bytes=VMEM_LIMIT)


def _tile(n, candidates, also_divides=0):
    for c in candidates:
        if n % c == 0 and also_divides % c == 0:
            return c
    return n


def _lambda_init(layer):
    return 0.8 - 0.6 * math.exp(-0.3 * layer)


def _rmsnorm_kernel(x_ref, g_ref, o_ref):
    x = x_ref[...]
    ms = jnp.mean(x * x, axis=-1, keepdims=True)
    o_ref[...] = (x * lax.rsqrt(ms + EPS) * g_ref[...]).astype(o_ref.dtype)


def _rmsnorm(x, g):
    m, d = x.shape
    tm = _tile(m, (256, 128, 64, 32, 16))
    return pl.pallas_call(
        _rmsnorm_kernel,
        grid=(m // tm,),
        in_specs=[pl.BlockSpec((tm, d), lambda i: (i, 0)),
                  pl.BlockSpec((1, d), lambda i: (0, 0))],
        out_specs=pl.BlockSpec((tm, d), lambda i: (i, 0)),
        out_shape=jax.ShapeDtypeStruct((m, d), BF16),
        compiler_params=_params(1),
        name="rmsnorm",
    )(x, g.reshape(1, d).astype(F32))


def _proj_kernel(a_ref, w_ref, g_ref, *out_refs, mode, scale, transposed):
    z = jnp.dot(a_ref[...], w_ref[...], preferred_element_type=F32)
    if mode == "u":
        out_refs[0][...] = z
    elif mode == "v":
        out_refs[0][...] = z
        if transposed:
            out_refs[1][0] = z.T.astype(BF16)
        else:
            out_refs[1][...] = z.astype(BF16)
    elif mode == "gate":
        out_refs[0][...] = jax.nn.sigmoid(z).astype(BF16)
    else:
        g = g_ref[...]
        hd = g.shape[-1]
        for c in range(z.shape[1] // hd):
            sl = slice(c * hd, (c + 1) * hd)
            zc = z[:, sl]
            y = zc * lax.rsqrt(jnp.mean(zc * zc, axis=-1, keepdims=True) + EPS) * g
            if mode == "k":
                out_refs[0][:, c, :] = y
                out_refs[1][:, sl] = y.astype(BF16)
            elif transposed:
                out_refs[0][sl, :] = (y * scale).T.astype(BF16)
            else:
                out_refs[0][:, sl] = (y * scale).astype(BF16)


def _proj(h, w, col_off, n, mode, gain=None, scale=1.0, transposed_tile=0):
    m, k = h.shape
    two_out = mode in ("k", "v")
    transposed = transposed_tile > 0
    if transposed and mode == "v":
        tm = transposed_tile
    else:
        tm = _tile(m, (1024, 512, 256, 128, 64, 32, 16))
    tn = _tile(n, (512,) if mode == "v" else (1024, 512, 256, 128), also_divides=col_off)
    off = col_off // tn
    if gain is None:
        gain = jnp.ones((1, 128), F32)
    gain = gain.reshape(1, -1).astype(F32)
    o_spec = pl.BlockSpec((tm, tn), lambda i, j: (i, j))
    if mode == "u":
        out_shape, out_specs = [jax.ShapeDtypeStruct((m, n), F32)], [o_spec]
    elif mode == "v" and transposed:
        out_shape = [jax.ShapeDtypeStruct((m, n), F32), jax.ShapeDtypeStruct((m // tm, n, tm), BF16)]
        out_specs = [o_spec, pl.BlockSpec((1, tn, tm), lambda i, j: (i, j, 0))]
    elif mode == "k":
        hd = gain.shape[-1]
        out_shape = [jax.ShapeDtypeStruct((m, n // hd, hd), F32), jax.ShapeDtypeStruct((m, n), BF16)]
        out_specs = [pl.BlockSpec((tm, tn // hd, hd), lambda i, j: (i, j, 0)), o_spec]
    elif two_out:
        out_shape = [jax.ShapeDtypeStruct((m, n), F32), jax.ShapeDtypeStruct((m, n), BF16)]
        out_specs = [o_spec, o_spec]
    elif transposed:
        out_shape = [jax.ShapeDtypeStruct((n, m), BF16)]
        out_specs = [pl.BlockSpec((tn, tm), lambda i, j: (j, i))]
    else:
        out_shape, out_specs = [jax.ShapeDtypeStruct((m, n), BF16)], [o_spec]
    outs = pl.pallas_call(
        functools.partial(_proj_kernel, mode=mode, scale=scale, transposed=transposed),
        grid=(m // tm, n // tn),
        in_specs=[pl.BlockSpec((tm, k), lambda i, j: (i, 0)),
                  pl.BlockSpec((k, tn), lambda i, j: (0, off + j)),
                  pl.BlockSpec(gain.shape, lambda i, j: (0, 0))],
        out_specs=out_specs,
        out_shape=out_shape,
        compiler_params=_params(2),
        name="proj_" + mode,
    )(h, w, gain)
    return outs if two_out else outs[0]


def _lambda(lq1, lk1, lq2, lk2, lam_i):
    a = jnp.sum(lq1[...] * lk1[...], axis=-1, keepdims=True)
    b = jnp.sum(lq2[...] * lk2[...], axis=-1, keepdims=True)
    return jnp.exp(a) - jnp.exp(b) + lam_i


def _head_out(o0, o1, lam, sub, lam_i):
    o = o0 - lam * o1
    y = o * lax.rsqrt(jnp.mean(o * o, axis=-1, keepdims=True) + EPS) * sub
    return y * (1.0 - lam_i)


def _nt_dot(a, b):
    return lax.dot_general(a, b, (((1,), (1,)), ((), ())), preferred_element_type=F32)


def _attn_prompt_kernel(qt_ref, k_ref, vt_ref, lq1, lk1, lq2, lk2, sub_ref, o_ref, acc_sc, sa_sc, sb_sc,
                        *, tq, tk, hd, lam_i):
    qi = pl.program_id(2)
    q1t = qt_ref[:hd, :]
    q2t = qt_ref[hd:, :]
    acc_sc[...] = jnp.zeros(acc_sc.shape, F32)

    def scores(j):
        k = k_ref[0, pl.ds(pl.multiple_of(j * tk, tk), tk), :]
        return jnp.concatenate([jnp.dot(k[:, :hd], q1t, preferred_element_type=F32),
                                jnp.dot(k[:, hd:], q2t, preferred_element_type=F32)], axis=1)

    def update(j, s, m_old, l_old):
        m_new = jnp.maximum(m_old, jnp.max(s, axis=0, keepdims=True))
        alpha = jnp.exp2(m_old - m_new)
        p = jnp.exp2(s - m_new)
        l_new = alpha * l_old + jnp.sum(p, axis=0, keepdims=True)
        acc_sc[...] = alpha * acc_sc[...] + jnp.dot(vt_ref[j], p.astype(BF16), preferred_element_type=F32)
        return m_new, l_new

    def put_scores(j, buf):
        s = scores(j)
        buf[...] = s

    def step(j, cur, nxt, carry):
        put_scores(j + 1, nxt)
        return update(j, cur[...], *carry)

    n_full = qi
    odd = n_full % 2

    @pl.when(odd == 0)
    def _():
        put_scores(0, sa_sc)

    @pl.when(odd == 1)
    def _():
        put_scores(0, sb_sc)

    init = (jnp.full((1, 2 * tq), -jnp.inf, F32), jnp.zeros((1, 2 * tq), F32))
    carry = lax.fori_loop(0, odd, lambda _, c: step(0, sb_sc, sa_sc, c), init)

    def pair(jj, c):
        j = odd + 2 * jj
        return step(j + 1, sb_sc, sa_sc, step(j, sa_sc, sb_sc, c))

    m, l = lax.fori_loop(0, n_full // 2, pair, carry)

    q_chunk = lax.broadcasted_iota(jnp.int32, (tk, tq), 1) // CHUNK
    k_chunk = lax.broadcasted_iota(jnp.int32, (tk, tq), 0) // CHUNK
    mask = k_chunk <= q_chunk
    s = jnp.where(jnp.concatenate([mask, mask], axis=1), sa_sc[...], jnp.finfo(F32).min)
    m, l = update(n_full, s, m, l)

    inv_l = 1.0 / l
    acc = acc_sc[...]
    lam = _lambda(lq1, lk1, lq2, lk2, lam_i)
    o0 = (acc[:, :tq] * inv_l[:, :tq]).T
    o1 = (acc[:, tq:] * inv_l[:, tq:]).T
    o_ref[0] = _head_out(o0, o1, lam, sub_ref[...], lam_i).astype(o_ref.dtype)


def _attn_prompt(qt, k, vt, lams, subln, n_heads, lam_i, tq, tk):
    b, t, qcols = k.shape
    hd = qcols // (2 * n_heads)
    dv = vt.shape[1] // n_heads
    nq = t // tq
    vec = pl.BlockSpec((1, hd), lambda bi, h, i: (0, 0))
    return pl.pallas_call(
        functools.partial(_attn_prompt_kernel, tq=tq, tk=tk, hd=hd, lam_i=lam_i),
        grid=(b, n_heads, nq),
        in_specs=[pl.BlockSpec((2 * hd, tq), lambda bi, h, i: (h, bi * nq + i)),
                  pl.BlockSpec((1, t, 2 * hd), lambda bi, h, i: (bi, 0, h)),
                  pl.BlockSpec((t // tk, dv, tk), lambda bi, h, i: (bi, h, 0)),
                  vec, vec, vec, vec,
                  pl.BlockSpec((1, dv), lambda bi, h, i: (0, 0))],
        out_specs=pl.BlockSpec((1, tq, dv), lambda bi, h, i: (bi, i, h)),
        out_shape=jax.ShapeDtypeStruct((b, t, n_heads * dv), BF16),
        scratch_shapes=[pltpu.VMEM((dv, 2 * tq), F32), pltpu.VMEM((tk, 2 * tq), F32),
                        pltpu.VMEM((tk, 2 * tq), F32)],
        compiler_params=_params(3),
        name="attn_prompt",
    )(qt, k, vt, *lams, subln)


def _attn_cached_kernel(q_ref, kc_ref, kn_ref, vc_ref, vn_ref, lq1, lk1, lq2, lk2, sub_ref, o_ref,
                        *, hd, lam_i):
    q = q_ref[0]
    ts = q.shape[0]
    kc = kc_ref[0].astype(BF16)
    kn = kn_ref[0]
    past = kc.shape[0]
    q_chunk = (past + lax.broadcasted_iota(jnp.int32, (ts, 1), 0)) // CHUNK
    mask_c = lax.broadcasted_iota(jnp.int32, (ts, past), 1) // CHUNK <= q_chunk
    mask_n = (past + lax.broadcasted_iota(jnp.int32, (ts, ts), 1)) // CHUNK <= q_chunk
    neg = jnp.finfo(F32).min
    outs = []
    for half in range(2):
        sl = slice(half * hd, (half + 1) * hd)
        sc = jnp.where(mask_c, _nt_dot(q[:, sl], kc[:, sl]), neg)
        sn = jnp.where(mask_n, _nt_dot(q[:, sl], kn[:, sl]), neg)
        m = jnp.maximum(jnp.max(sc, axis=-1, keepdims=True), jnp.max(sn, axis=-1, keepdims=True))
        pc = jnp.exp(sc - m)
        pn = jnp.exp(sn - m)
        l = jnp.sum(pc, axis=-1, keepdims=True) + jnp.sum(pn, axis=-1, keepdims=True)
        o = (jnp.dot(pc.astype(BF16), vc_ref[0].astype(BF16), preferred_element_type=F32)
             + jnp.dot(pn.astype(BF16), vn_ref[0], preferred_element_type=F32))
        outs.append(o * (1.0 / l))
    lam = _lambda(lq1, lk1, lq2, lk2, lam_i)
    o_ref[0] = _head_out(outs[0], outs[1], lam, sub_ref[...], lam_i).astype(o_ref.dtype)


def _attn_cached(q, k_cache, k_new, v_cache, v_new, lams, subln, n_heads, lam_i):
    b, ts, qcols = q.shape
    past = k_cache.shape[1]
    hd = qcols // (2 * n_heads)
    dv = v_new.shape[-1] // n_heads
    vec = pl.BlockSpec((1, hd), lambda bi, h: (0, 0))
    return pl.pallas_call(
        functools.partial(_attn_cached_kernel, hd=hd, lam_i=lam_i),
        grid=(b, n_heads),
        in_specs=[pl.BlockSpec((1, ts, 2 * hd), lambda bi, h: (bi, 0, h)),
                  pl.BlockSpec((1, past, 2 * hd), lambda bi, h: (bi, 0, h)),
                  pl.BlockSpec((1, ts, 2 * hd), lambda bi, h: (bi, 0, h)),
                  pl.BlockSpec((1, past, dv), lambda bi, h: (bi, 0, h)),
                  pl.BlockSpec((1, ts, dv), lambda bi, h: (bi, 0, h)),
                  vec, vec, vec, vec,
                  pl.BlockSpec((1, dv), lambda bi, h: (0, 0))],
        out_specs=pl.BlockSpec((1, ts, dv), lambda bi, h: (bi, 0, h)),
        out_shape=jax.ShapeDtypeStruct((b, ts, n_heads * dv), BF16),
        compiler_params=_params(2),
        name="attn_cached",
    )(q, k_cache, k_new, v_cache, v_new, *lams, subln)


def _poolmix_kernel(u_ref, uprev_ref, hist0_ref, wp_ref, ps_ref, ga_ref, gb_ref, at_ref, o_ref, ext_sc,
                    *, tm, pos0):
    i = pl.program_id(1)
    hist = jnp.where(i == 0, hist0_ref[0], uprev_ref[0])
    ext_sc[0:HIST_ROWS, :] = hist
    ext_sc[HIST_ROWS:HIST_ROWS + tm, :] = u_ref[0]
    pos = pos0 + i * tm + lax.broadcasted_iota(jnp.int32, (tm, 1), 0)
    gin = wp_ref.shape[1]
    gout = wp_ref.shape[2]
    for g, w in enumerate(POOL_WINDOWS):
        cin = slice(g * gin, (g + 1) * gin)
        cout = slice(g * gout, (g + 1) * gout)
        u_new = ext_sc[HIST_ROWS:HIST_ROWS + tm, cin]
        win = u_new
        for j in range(1, w):
            win = win + ext_sc[HIST_ROWS - j:HIST_ROWS - j + tm, cin]
        inv_cnt = 1.0 / jnp.minimum(pos + 1, w).astype(F32)
        pooled = win * inv_cnt - u_new
        y = jnp.dot(pooled.astype(BF16), wp_ref[g], preferred_element_type=F32) * ps_ref[:, cout]
        mix = ga_ref[0, :, cout].astype(F32) * y + gb_ref[0, :, cout].astype(F32) * at_ref[0, :, cout].astype(F32)
        o_ref[0, :, cout] = mix.astype(o_ref.dtype)


def _poolmix(u, hist0, w_pool, pool_scale, gates, attn, pos0):
    b, t, pin = u.shape
    d = attn.shape[-1]
    tm = _tile(t, (256, 128, 64, 32, 16))
    hb = tm // HIST_ROWS
    return pl.pallas_call(
        functools.partial(_poolmix_kernel, tm=tm, pos0=pos0),
        grid=(b, t // tm),
        in_specs=[pl.BlockSpec((1, tm, pin), lambda bi, i: (bi, i, 0)),
                  pl.BlockSpec((1, HIST_ROWS, pin), lambda bi, i: (bi, jnp.maximum(i * hb - 1, 0), 0)),
                  pl.BlockSpec((1, HIST_ROWS, pin), lambda bi, i: (bi, 0, 0)),
                  pl.BlockSpec(w_pool.shape, lambda bi, i: (0, 0, 0)),
                  pl.BlockSpec((1, d), lambda bi, i: (0, 0)),
                  pl.BlockSpec((1, tm, d), lambda bi, i: (bi, i, 0)),
                  pl.BlockSpec((1, tm, d), lambda bi, i: (bi, i, 1)),
                  pl.BlockSpec((1, tm, d), lambda bi, i: (bi, i, 0))],
        out_specs=pl.BlockSpec((1, tm, d), lambda bi, i: (bi, i, 0)),
        out_shape=jax.ShapeDtypeStruct((b, t, d), BF16),
        scratch_shapes=[pltpu.VMEM((HIST_ROWS + tm, pin), F32)],
        compiler_params=_params(2),
        name="poolmix",
    )(u, u, hist0, w_pool, pool_scale.reshape(1, d).astype(F32), gates, gates, attn)


def _mm_res_kernel(a_ref, w_ref, r_ref, o_ref):
    o_ref[...] = r_ref[...] + jnp.dot(a_ref[...], w_ref[...], preferred_element_type=F32)


def _mm_res(a, w, res):
    m, k = a.shape
    n = w.shape[1]
    big_k = k > 8192
    tm = _tile(m, (512,) if big_k else (1024, 512, 256, 128, 64, 32, 16))
    tn = _tile(n, (256,) if big_k else (1024, 512, 256, 128))
    return pl.pallas_call(
        _mm_res_kernel,
        grid=(m // tm, n // tn),
        in_specs=[pl.BlockSpec((tm, k), lambda i, j: (i, 0)),
                  pl.BlockSpec((k, tn), lambda i, j: (0, j)),
                  pl.BlockSpec((tm, tn), lambda i, j: (i, j))],
        out_specs=pl.BlockSpec((tm, tn), lambda i, j: (i, j)),
        out_shape=jax.ShapeDtypeStruct((m, n), F32),
        compiler_params=_params(2),
        name="mm_res",
    )(a, w, res)


def _swiglu_kernel(a_ref, wg_ref, wu_ref, o_ref):
    a = a_ref[...]
    gate = jnp.dot(a, wg_ref[...], preferred_element_type=F32)
    up = jnp.dot(a, wu_ref[...], preferred_element_type=F32)
    o_ref[...] = (gate * jax.nn.sigmoid(gate) * up).astype(o_ref.dtype)


def _swiglu(h, w, d_ff):
    m, k = h.shape
    tm = _tile(m, (1024, 512, 256, 128, 64, 32, 16))
    tn = _tile(d_ff, (256, 128))
    nb = d_ff // tn
    return pl.pallas_call(
        _swiglu_kernel,
        grid=(m // tm, nb),
        in_specs=[pl.BlockSpec((tm, k), lambda i, j: (i, 0)),
                  pl.BlockSpec((k, tn), lambda i, j: (0, j)),
                  pl.BlockSpec((k, tn), lambda i, j: (0, nb + j))],
        out_specs=pl.BlockSpec((tm, tn), lambda i, j: (i, j)),
        out_shape=jax.ShapeDtypeStruct((m, d_ff), BF16),
        compiler_params=_params(2),
        name="swiglu",
    )(h, w, w)


def _layer(x, cache, layer, p):
    b, t, d = x.shape
    n_heads, hd, dv = p["n_heads"], p["hd"], p["dv"]
    pool_in = p["w_pool"].shape[0] * p["w_pool"].shape[1]
    attn_qk = n_heads * 2 * hd
    attn_v = n_heads * dv
    o1 = pool_in
    o2 = o1 + attn_qk
    o3 = o2 + attn_qk
    o4 = o3 + attn_v
    lam_i = _lambda_init(layer)
    x2 = x.reshape(b * t, d)

    h = _rmsnorm(x2, p["norm_mix"])
    u = _proj(h, p["w_in"], 0, pool_in, "u")
    k32, k16 = _proj(h, p["w_in"], o2, attn_qk, "k", gain=p["k_norm"])
    gates = _proj(h, p["w_in"], o4, 2 * d, "gate")
    u3 = u.reshape(b, t, pool_in)
    k3 = k16.reshape(b, t, attn_qk)
    if cache is None:
        pos0 = 0
        hist = jnp.zeros((b, POOL_HIST, pool_in), F32)
        tq = _tile(t, (512, 256, 128, 64))
        tk = tq
        qt = _proj(h, p["w_in"], o1, attn_qk, "q", gain=p["q_norm"], scale=hd ** -0.5 * math.log2(math.e),
                   transposed_tile=tq)
        v32, vt = _proj(h, p["w_in"], o3, attn_v, "v", transposed_tile=tk)
        attn = _attn_prompt(qt, k3, vt, p["lams"], p["subln"], n_heads, lam_i, tq, tk)
    else:
        cache_pool, cache_k, cache_v = cache
        pos0 = cache_k.shape[1]
        hist = cache_pool
        q = _proj(h, p["w_in"], o1, attn_qk, "q", gain=p["q_norm"], scale=hd ** -0.5)
        v32, v16 = _proj(h, p["w_in"], o3, attn_v, "v")
        attn = _attn_cached(q.reshape(b, t, attn_qk), cache_k.reshape(b, pos0, attn_qk), k3,
                            cache_v.reshape(b, pos0, attn_v), v16.reshape(b, t, attn_v),
                            p["lams"], p["subln"], n_heads, lam_i)
    hist0 = jnp.pad(hist, ((0, 0), (HIST_ROWS - POOL_HIST, 0), (0, 0)))
    mix = _poolmix(u3, hist0, p["w_pool"], p["pool_scale"], gates.reshape(b, t, 2 * d), attn, pos0)

    x1 = _mm_res(mix.reshape(b * t, d), p["w_out"], x2)
    h2 = _rmsnorm(x1, p["norm_ffn"])
    act = _swiglu(h2, p["w_ffn_in"], p["d_ff"])
    y = _mm_res(act, p["w_ffn_out"], x1)

    new_pool = jnp.concatenate([hist, u3], axis=1)[:, -POOL_HIST:] if t < POOL_HIST else u3[:, t - POOL_HIST:]
    new_k = k32.reshape(b, t, n_heads, 2, hd)
    new_v = v32.reshape(b, t, n_heads, dv)
    return y.reshape(b, t, d), new_pool, new_k, new_v


def kernel(x_prompt, x_sample, cache_pool, cache_k, cache_v, norm_mix, w_in, w_pool, pool_scale, q_norm, k_norm,
           lambda_q1, lambda_k1, lambda_q2, lambda_k2, subln, w_out, norm_ffn, w_ffn_in, w_ffn_out):
    depth = w_in.shape[0]
    n_heads, hd, dv = cache_k.shape[3], cache_k.shape[5], cache_v.shape[4]
    xp, xs = x_prompt, x_sample
    outs_p, outs_s = [], []
    for l in range(depth):
        vec = lambda a: a[l].reshape(1, -1).astype(F32)
        p = dict(
            n_heads=n_heads, hd=hd, dv=dv, d_ff=w_ffn_out.shape[1],
            norm_mix=norm_mix[l], norm_ffn=norm_ffn[l], q_norm=q_norm[l], k_norm=k_norm[l],
            pool_scale=pool_scale[l], subln=vec(subln),
            lams=(vec(lambda_q1), vec(lambda_k1), vec(lambda_q2), vec(lambda_k2)),
            w_in=w_in[l].astype(BF16), w_pool=w_pool[l].astype(BF16), w_out=w_out[l].astype(BF16),
            w_ffn_in=w_ffn_in[l].astype(BF16), w_ffn_out=w_ffn_out[l].astype(BF16),
        )
        xp, pool_p, k_p, v_p = _layer(xp, None, l, p)
        xs, pool_s, k_s, v_s = _layer(xs, (cache_pool[l], cache_k[l], cache_v[l]), l, p)
        outs_p.append((pool_p, k_p, v_p))
        outs_s.append((pool_s, k_s, v_s))
    stack = lambda outs, i: outs[0][i][None] if depth == 1 else jnp.stack([o[i] for o in outs])
    return (xp, xs, stack(outs_p, 0), stack(outs_p, 1), stack(outs_p, 2),
            stack(outs_s, 0), stack(outs_s, 1), stack(outs_s, 2))
```

```python
import functools
import math

import jax
import jax.numpy as jnp
from jax import lax
from jax.experimental import pallas as pl
from jax.experimental.pallas import tpu as pltpu

F32 = jnp.float32
BF16 = jnp.bfloat16

CHUNK = 64
POOL_WINDOWS = (2, 4, 8, 16)
POOL_HIST = max(POOL_WINDOWS) - 1
HIST_ROWS = 16
EPS = 1e-6
LANES = 128
BF16_SUBLANES = 16

V7X_VMEM_BYTES = 64 * 1024 * 1024
VMEM_LIMIT = V7X_VMEM_BYTES - 6 * 1024 * 1024


def _params(n_grid_axes):
    return pltpu.CompilerParams(dimension_semantics=("arbitrary",) * n_grid_axes,
                                vmem_limit_bytes=VMEM_LIMIT)


def _tile(n, candidates):
    for c in candidates:
        if n % c == 0:
            return c
    return n


def _lambda_init(layer):
    return 0.8 - 0.6 * math.exp(-0.3 * layer)


def _with_side_casts(body, n_in, n_side):
    def side_cast_kernel(*refs):
        ins, srcs, rest = refs[:n_in], refs[n_in:n_in + n_side], refs[n_in + n_side:]
        outs, dsts = rest[:len(rest) - n_side], rest[len(rest) - n_side:]
        body(*ins, *outs)
        for src, dst in zip(srcs, dsts):
            dst[...] = src[...].astype(dst.dtype)
    return side_cast_kernel


def _side_casts(jobs, grid):
    n_steps = math.prod(grid)

    def step_of(*g):
        step = g[0]
        for extent, idx in zip(grid[1:], g[1:]):
            step = step * extent + idx
        return step

    srcs, in_specs, out_specs, out_shapes, riding = [], [], [], [], []
    for src, col0, ncols in jobs:
        r = src.shape[0]
        wb = math.gcd(col0, ncols) if col0 else ncols
        ncb = ncols // wb
        rows = r * ncb // n_steps
        fits = (n_steps % ncb == 0 and r % (n_steps // ncb) == 0
                and rows % BF16_SUBLANES == 0 and wb % LANES == 0)
        riding.append(fits)
        if not fits:
            continue
        cb0 = col0 // wb
        srcs.append(src)
        in_specs.append(pl.BlockSpec(
            (rows, wb), lambda *g, ncb=ncb, cb0=cb0: (step_of(*g) // ncb, cb0 + step_of(*g) % ncb)))
        out_specs.append(pl.BlockSpec((rows, wb), lambda *g, ncb=ncb: (step_of(*g) // ncb, step_of(*g) % ncb)))
        out_shapes.append(jax.ShapeDtypeStruct((r, ncols), BF16))

    def finish(results):
        results = list(results)
        return [results.pop(0) if fits else src[:, col0:col0 + ncols].astype(BF16)
                for fits, (src, col0, ncols) in zip(riding, jobs)]

    return srcs, in_specs, out_specs, out_shapes, finish


def _rmsnorm_kernel(x_ref, g_ref, o_ref):
    x = x_ref[...]
    ms = jnp.mean(x * x, axis=-1, keepdims=True)
    o_ref[...] = (x * lax.rsqrt(ms + EPS) * g_ref[...]).astype(o_ref.dtype)


def _rmsnorm(x, g):
    m, d = x.shape
    tm = _tile(m, (256, 128, 64, 32, 16))
    return pl.pallas_call(
        _rmsnorm_kernel,
        grid=(m // tm,),
        in_specs=[pl.BlockSpec((tm, d), lambda i: (i, 0)),
                  pl.BlockSpec((1, d), lambda i: (0, 0))],
        out_specs=pl.BlockSpec((tm, d), lambda i: (i, 0)),
        out_shape=jax.ShapeDtypeStruct((m, d), BF16),
        compiler_params=_params(1),
        name="rmsnorm",
    )(x, g.reshape(1, d).astype(F32))


def _proj_kernel(a_ref, w_ref, g_ref, *out_refs, mode, scale, transposed):
    z = jnp.dot(a_ref[...], w_ref[...], preferred_element_type=F32)
    if mode == "u":
        out_refs[0][...] = z
    elif mode == "v":
        out_refs[0][...] = z
        if transposed:
            rows = out_refs[1].shape[-1]
            for r in range(out_refs[1].shape[0]):
                out_refs[1][r] = z[r * rows:(r + 1) * rows, :].T.astype(BF16)
        else:
            out_refs[1][...] = z.astype(BF16)
    elif mode == "gate":
        out_refs[0][...] = jax.nn.sigmoid(z).astype(BF16)
    else:
        g = g_ref[...]
        hd = g.shape[-1]
        for c in range(z.shape[1] // hd):
            sl = slice(c * hd, (c + 1) * hd)
            zc = z[:, sl]
            y = zc * lax.rsqrt(jnp.mean(zc * zc, axis=-1, keepdims=True) + EPS) * g
            if mode == "k":
                out_refs[0][:, c, :] = y
                out_refs[1][:, sl] = y.astype(BF16)
            elif transposed:
                out_refs[0][sl, :] = (y * scale).T.astype(BF16)
            else:
                out_refs[0][:, sl] = (y * scale).astype(BF16)


def _proj(h, w, n, mode, gain=None, scale=1.0, transposed_tile=0, side_jobs=()):
    m, k = h.shape
    two_out = mode in ("k", "v")
    transposed = transposed_tile > 0
    tm = _tile(m, (512,) if mode == "k" else (1024, 512, 256, 128, 64, 32, 16))
    tn = _tile(n, (512,) if mode == "v" else (1024, 512, 256, 128))
    slabs = tm // transposed_tile if transposed else 1
    if gain is None:
        gain = jnp.ones((1, 128), F32)
    gain = gain.reshape(1, -1).astype(F32)
    o_spec = pl.BlockSpec((tm, tn), lambda i, j: (i, j))
    if mode == "u":
        out_shape, out_specs = [jax.ShapeDtypeStruct((m, n), F32)], [o_spec]
    elif mode == "v" and transposed:
        out_shape = [jax.ShapeDtypeStruct((m, n), F32),
                     jax.ShapeDtypeStruct((m // transposed_tile, n, transposed_tile), BF16)]
        out_specs = [o_spec, pl.BlockSpec((slabs, tn, transposed_tile), lambda i, j: (i, j, 0))]
    elif mode == "k":
        hd = gain.shape[-1]
        out_shape = [jax.ShapeDtypeStruct((m, n // hd, hd), F32), jax.ShapeDtypeStruct((m, n), BF16)]
        out_specs = [pl.BlockSpec((tm, tn // hd, hd), lambda i, j: (i, j, 0)), o_spec]
    elif two_out:
        out_shape = [jax.ShapeDtypeStruct((m, n), F32), jax.ShapeDtypeStruct((m, n), BF16)]
        out_specs = [o_spec, o_spec]
    elif transposed:
        out_shape = [jax.ShapeDtypeStruct((n, m), BF16)]
        out_specs = [pl.BlockSpec((tn, tm), lambda i, j: (j, i))]
    else:
        out_shape, out_specs = [jax.ShapeDtypeStruct((m, n), BF16)], [o_spec]
    grid = (m // tm, n // tn)
    srcs, side_in, side_out, side_shapes, finish = _side_casts(side_jobs, grid)
    n_main = len(out_shape)
    outs = pl.pallas_call(
        _with_side_casts(functools.partial(_proj_kernel, mode=mode, scale=scale, transposed=transposed),
                         3, len(srcs)),
        grid=grid,
        in_specs=[pl.BlockSpec((tm, k), lambda i, j: (i, 0)),
                  pl.BlockSpec((k, tn), lambda i, j: (0, j)),
                  pl.BlockSpec(gain.shape, lambda i, j: (0, 0))] + side_in,
        out_specs=out_specs + side_out,
        out_shape=out_shape + side_shapes,
        compiler_params=_params(2),
        name="proj_" + mode,
    )(h, w, gain, *srcs)
    main = tuple(outs[:n_main]) if two_out else outs[0]
    return main, finish(outs[n_main:])


def _lambda(lq1, lk1, lq2, lk2, lam_i):
    a = jnp.sum(lq1[...] * lk1[...], axis=-1, keepdims=True)
    b = jnp.sum(lq2[...] * lk2[...], axis=-1, keepdims=True)
    return jnp.exp(a) - jnp.exp(b) + lam_i


def _head_out(o0, o1, lam, sub, lam_i):
    o = o0 - lam * o1
    y = o * lax.rsqrt(jnp.mean(o * o, axis=-1, keepdims=True) + EPS) * sub
    return y * (1.0 - lam_i)


def _nt_dot(a, b):
    return lax.dot_general(a, b, (((1,), (1,)), ((), ())), preferred_element_type=F32)


def _attn_prompt_kernel(qt_ref, k_ref, vt_ref, lq1, lk1, lq2, lk2, sub_ref, o_ref, acc_sc, sa_sc, sb_sc,
                        *, tq, tk, hd, lam_i):
    qi = pl.program_id(2)
    q1t = qt_ref[:hd, :]
    q2t = qt_ref[hd:, :]
    acc_sc[...] = jnp.zeros(acc_sc.shape, F32)

    def scores(j):
        k = k_ref[0, pl.ds(pl.multiple_of(j * tk, tk), tk), :]
        return jnp.concatenate([jnp.dot(k[:, :hd], q1t, preferred_element_type=F32),
                                jnp.dot(k[:, hd:], q2t, preferred_element_type=F32)], axis=1)

    def update(j, s, m_old, l_old):
        m_new = jnp.maximum(m_old, jnp.max(s, axis=0, keepdims=True))
        alpha = jnp.exp2(m_old - m_new)
        p = jnp.exp2(s - m_new)
        l_new = alpha * l_old + jnp.sum(p, axis=0, keepdims=True)
        acc_sc[...] = alpha * acc_sc[...] + jnp.dot(vt_ref[j], p.astype(BF16), preferred_element_type=F32)
        return m_new, l_new

    def put_scores(j, buf):
        s = scores(j)
        buf[...] = s

    def step(j, cur, nxt, carry):
        put_scores(j + 1, nxt)
        return update(j, cur[...], *carry)

    n_full = qi
    odd = n_full % 2

    @pl.when(odd == 0)
    def _():
        put_scores(0, sa_sc)

    @pl.when(odd == 1)
    def _():
        put_scores(0, sb_sc)

    init = (jnp.full((1, 2 * tq), -jnp.inf, F32), jnp.zeros((1, 2 * tq), F32))
    carry = lax.fori_loop(0, odd, lambda _, c: step(0, sb_sc, sa_sc, c), init)

    def pair(jj, c):
        j = odd + 2 * jj
        return step(j + 1, sb_sc, sa_sc, step(j, sa_sc, sb_sc, c))

    m, l = lax.fori_loop(0, n_full // 2, pair, carry)

    q_chunk = lax.broadcasted_iota(jnp.int32, (tk, tq), 1) // CHUNK
    k_chunk = lax.broadcasted_iota(jnp.int32, (tk, tq), 0) // CHUNK
    mask = k_chunk <= q_chunk
    s = jnp.where(jnp.concatenate([mask, mask], axis=1), sa_sc[...], jnp.finfo(F32).min)
    m, l = update(n_full, s, m, l)

    inv_l = 1.0 / l
    acc = acc_sc[...]
    lam = _lambda(lq1, lk1, lq2, lk2, lam_i)
    o0 = (acc[:, :tq] * inv_l[:, :tq]).T
    o1 = (acc[:, tq:] * inv_l[:, tq:]).T
    o_ref[0] = _head_out(o0, o1, lam, sub_ref[...], lam_i).astype(o_ref.dtype)


def _attn_prompt(qt, k, vt, lams, subln, n_heads, lam_i, tq, tk):
    b, t, qcols = k.shape
    hd = qcols // (2 * n_heads)
    dv = vt.shape[1] // n_heads
    nq = t // tq
    vec = pl.BlockSpec((1, hd), lambda bi, h, i: (0, 0))
    return pl.pallas_call(
        functools.partial(_attn_prompt_kernel, tq=tq, tk=tk, hd=hd, lam_i=lam_i),
        grid=(b, n_heads, nq),
        in_specs=[pl.BlockSpec((2 * hd, tq), lambda bi, h, i: (h, bi * nq + i)),
                  pl.BlockSpec((1, t, 2 * hd), lambda bi, h, i: (bi, 0, h)),
                  pl.BlockSpec((t // tk, dv, tk), lambda bi, h, i: (bi, h, 0)),
                  vec, vec, vec, vec,
                  pl.BlockSpec((1, dv), lambda bi, h, i: (0, 0))],
        out_specs=pl.BlockSpec((1, tq, dv), lambda bi, h, i: (bi, i, h)),
        out_shape=jax.ShapeDtypeStruct((b, t, n_heads * dv), BF16),
        scratch_shapes=[pltpu.VMEM((dv, 2 * tq), F32), pltpu.VMEM((tk, 2 * tq), F32),
                        pltpu.VMEM((tk, 2 * tq), F32)],
        compiler_params=_params(3),
        name="attn_prompt",
    )(qt, k, vt, *lams, subln)


def _attn_cached_kernel(q_ref, kc_ref, kn_ref, vc_ref, vn_ref, lq1, lk1, lq2, lk2, sub_ref, o_ref,
                        *, hd, lam_i):
    q = q_ref[0]
    ts = q.shape[0]
    kc = kc_ref[0].astype(BF16)
    kn = kn_ref[0]
    past = kc.shape[0]
    q_chunk = (past + lax.broadcasted_iota(jnp.int32, (ts, 1), 0)) // CHUNK
    mask_c = lax.broadcasted_iota(jnp.int32, (ts, past), 1) // CHUNK <= q_chunk
    mask_n = (past + lax.broadcasted_iota(jnp.int32, (ts, ts), 1)) // CHUNK <= q_chunk
    neg = jnp.finfo(F32).min
    outs = []
    for half in range(2):
        sl = slice(half * hd, (half + 1) * hd)
        sc = jnp.where(mask_c, _nt_dot(q[:, sl], kc[:, sl]), neg)
        sn = jnp.where(mask_n, _nt_dot(q[:, sl], kn[:, sl]), neg)
        m = jnp.maximum(jnp.max(sc, axis=-1, keepdims=True), jnp.max(sn, axis=-1, keepdims=True))
        pc = jnp.exp(sc - m)
        pn = jnp.exp(sn - m)
        l = jnp.sum(pc, axis=-1, keepdims=True) + jnp.sum(pn, axis=-1, keepdims=True)
        o = (jnp.dot(pc.astype(BF16), vc_ref[0].astype(BF16), preferred_element_type=F32)
             + jnp.dot(pn.astype(BF16), vn_ref[0], preferred_element_type=F32))
        outs.append(o * (1.0 / l))
    lam = _lambda(lq1, lk1, lq2, lk2, lam_i)
    o_ref[0] = _head_out(outs[0], outs[1], lam, sub_ref[...], lam_i).astype(o_ref.dtype)


def _attn_cached(q, k_cache, k_new, v_cache, v_new, lams, subln, n_heads, lam_i):
    b, ts, qcols = q.shape
    past = k_cache.shape[1]
    hd = qcols // (2 * n_heads)
    dv = v_new.shape[-1] // n_heads
    vec = pl.BlockSpec((1, hd), lambda bi, h: (0, 0))
    return pl.pallas_call(
        functools.partial(_attn_cached_kernel, hd=hd, lam_i=lam_i),
        grid=(b, n_heads),
        in_specs=[pl.BlockSpec((1, ts, 2 * hd), lambda bi, h: (bi, 0, h)),
                  pl.BlockSpec((1, past, 2 * hd), lambda bi, h: (bi, 0, h)),
                  pl.BlockSpec((1, ts, 2 * hd), lambda bi, h: (bi, 0, h)),
                  pl.BlockSpec((1, past, dv), lambda bi, h: (bi, 0, h)),
                  pl.BlockSpec((1, ts, dv), lambda bi, h: (bi, 0, h)),
                  vec, vec, vec, vec,
                  pl.BlockSpec((1, dv), lambda bi, h: (0, 0))],
        out_specs=pl.BlockSpec((1, ts, dv), lambda bi, h: (bi, 0, h)),
        out_shape=jax.ShapeDtypeStruct((b, ts, n_heads * dv), BF16),
        compiler_params=_params(2),
        name="attn_cached",
    )(q, k_cache, k_new, v_cache, v_new, *lams, subln)


def _poolmix_kernel(u_ref, uprev_ref, hist0_ref, wp_ref, ps_ref, ga_ref, gb_ref, at_ref, o_ref, ext_sc,
                    *, tm, pos0):
    i = pl.program_id(1)
    hist = jnp.where(i == 0, hist0_ref[0], uprev_ref[0])
    ext_sc[0:HIST_ROWS, :] = hist
    ext_sc[HIST_ROWS:HIST_ROWS + tm, :] = u_ref[0]
    pos = pos0 + i * tm + lax.broadcasted_iota(jnp.int32, (tm, 1), 0)
    gin = wp_ref.shape[1]
    gout = wp_ref.shape[2]
    for g, w in enumerate(POOL_WINDOWS):
        cin = slice(g * gin, (g + 1) * gin)
        cout = slice(g * gout, (g + 1) * gout)
        u_new = ext_sc[HIST_ROWS:HIST_ROWS + tm, cin]
        win = u_new
        for j in range(1, w):
            win = win + ext_sc[HIST_ROWS - j:HIST_ROWS - j + tm, cin]
        inv_cnt = 1.0 / jnp.minimum(pos + 1, w).astype(F32)
        pooled = win * inv_cnt - u_new
        y = jnp.dot(pooled.astype(BF16), wp_ref[g], preferred_element_type=F32) * ps_ref[:, cout]
        mix = ga_ref[0, :, cout].astype(F32) * y + gb_ref[0, :, cout].astype(F32) * at_ref[0, :, cout].astype(F32)
        o_ref[0, :, cout] = mix.astype(o_ref.dtype)


def _poolmix(u, hist0, w_pool, pool_scale, gates, attn, pos0):
    b, t, pin = u.shape
    d = attn.shape[-1]
    tm = _tile(t, (256, 128, 64, 32, 16))
    hb = tm // HIST_ROWS
    return pl.pallas_call(
        functools.partial(_poolmix_kernel, tm=tm, pos0=pos0),
        grid=(b, t // tm),
        in_specs=[pl.BlockSpec((1, tm, pin), lambda bi, i: (bi, i, 0)),
                  pl.BlockSpec((1, HIST_ROWS, pin), lambda bi, i: (bi, jnp.maximum(i * hb - 1, 0), 0)),
                  pl.BlockSpec((1, HIST_ROWS, pin), lambda bi, i: (bi, 0, 0)),
                  pl.BlockSpec(w_pool.shape, lambda bi, i: (0, 0, 0)),
                  pl.BlockSpec((1, d), lambda bi, i: (0, 0)),
                  pl.BlockSpec((1, tm, d), lambda bi, i: (bi, i, 0)),
                  pl.BlockSpec((1, tm, d), lambda bi, i: (bi, i, 1)),
                  pl.BlockSpec((1, tm, d), lambda bi, i: (bi, i, 0))],
        out_specs=pl.BlockSpec((1, tm, d), lambda bi, i: (bi, i, 0)),
        out_shape=jax.ShapeDtypeStruct((b, t, d), BF16),
        scratch_shapes=[pltpu.VMEM((HIST_ROWS + tm, pin), F32)],
        compiler_params=_params(2),
        name="poolmix",
    )(u, u, hist0, w_pool, pool_scale.reshape(1, d).astype(F32), gates, gates, attn)


def _mm_res_kernel(a_ref, w_ref, r_ref, o_ref):
    o_ref[...] = r_ref[...] + jnp.dot(a_ref[...], w_ref[...], preferred_element_type=F32)


def _mm_res(a, w, res):
    m, k = a.shape
    n = w.shape[1]
    big_k = k > 8192
    tm = _tile(m, (512,) if big_k else (1024, 512, 256, 128, 64, 32, 16))
    tn = _tile(n, (512, 256) if big_k else (1024, 512, 256, 128))
    return pl.pallas_call(
        _mm_res_kernel,
        grid=(m // tm, n // tn),
        in_specs=[pl.BlockSpec((tm, k), lambda i, j: (i, 0)),
                  pl.BlockSpec((k, tn), lambda i, j: (0, j)),
                  pl.BlockSpec((tm, tn), lambda i, j: (i, j))],
        out_specs=pl.BlockSpec((tm, tn), lambda i, j: (i, j)),
        out_shape=jax.ShapeDtypeStruct((m, n), F32),
        compiler_params=_params(2),
        name="mm_res",
    )(a, w, res)


def _swiglu_kernel(a_ref, wg_ref, wu_ref, o_ref):
    a = a_ref[...]
    gate = jnp.dot(a, wg_ref[...], preferred_element_type=F32)
    up = jnp.dot(a, wu_ref[...], preferred_element_type=F32)
    o_ref[...] = (gate * jax.nn.sigmoid(gate) * up).astype(o_ref.dtype)


def _swiglu(h, wg, wu, side_jobs=()):
    m, k = h.shape
    d_ff = wg.shape[1]
    tm = _tile(m, (1024, 512, 256, 128, 64, 32, 16))
    tn = _tile(d_ff, (256, 128))
    grid = (m // tm, d_ff // tn)
    srcs, side_in, side_out, side_shapes, finish = _side_casts(side_jobs, grid)
    w_spec = pl.BlockSpec((k, tn), lambda i, j: (0, j))
    outs = pl.pallas_call(
        _with_side_casts(_swiglu_kernel, 3, len(srcs)),
        grid=grid,
        in_specs=[pl.BlockSpec((tm, k), lambda i, j: (i, 0)), w_spec, w_spec] + side_in,
        out_specs=[pl.BlockSpec((tm, tn), lambda i, j: (i, j))] + side_out,
        out_shape=[jax.ShapeDtypeStruct((m, d_ff), BF16)] + side_shapes,
        compiler_params=_params(2),
        name="swiglu",
    )(h, wg, wu, *srcs)
    return outs[0], finish(outs[1:])


def _layer(x, cache, layer, p, wts):
    b, t, d = x.shape
    n_heads, hd, dv = p["n_heads"], p["hd"], p["dv"]
    pool_in = p["w_pool"].shape[0] * p["w_pool"].shape[1]
    attn_qk = n_heads * 2 * hd
    attn_v = n_heads * dv
    d_ff = p["w_ffn_out"].shape[0]
    o1 = pool_in
    o2 = o1 + attn_qk
    o3 = o2 + attn_qk
    o4 = o3 + attn_v
    lam_i = _lambda_init(layer)
    x2 = x.reshape(b * t, d)
    w_in, w_ffn_in = p["w_in"], p["w_ffn_in"]

    def jobs(*specs):
        todo = [sp for sp in specs if sp[0] not in wts]
        return [sp[0] for sp in todo], [sp[1:] for sp in todo]

    def keep(names, cast):
        wts.update(zip(names, cast))

    if "u" not in wts:
        wts["u"] = w_in[:, :o1].astype(BF16)
        wts["pool"] = p["w_pool"].astype(BF16)
    h = _rmsnorm(x2, p["norm_mix"])
    names, side = jobs(("k", w_in, o2, attn_qk))
    u, cast = _proj(h, wts["u"], pool_in, "u", side_jobs=side)
    keep(names, cast)
    names, side = jobs(("gate", w_in, o4, 2 * d))
    (k32, k16), cast = _proj(h, wts["k"], attn_qk, "k", gain=p["k_norm"], side_jobs=side)
    keep(names, cast)
    names, side = jobs(("q", w_in, o1, attn_qk), ("v", w_in, o3, attn_v), ("out", p["w_out"], 0, d))
    gates, cast = _proj(h, wts["gate"], 2 * d, "gate", side_jobs=side)
    keep(names, cast)
    u3 = u.reshape(b, t, pool_in)
    k3 = k16.reshape(b, t, attn_qk)
    names_q, side_q = jobs(("ffn_gate", w_ffn_in, 0, d_ff))
    names_v, side_v = jobs(("ffn_up", w_ffn_in, d_ff, d_ff))
    if cache is None:
        pos0 = 0
        hist = jnp.zeros((b, POOL_HIST, pool_in), F32)
        tq = _tile(t, (512, 256, 128, 64))
        tk = tq
        qt, cast = _proj(h, wts["q"], attn_qk, "q", gain=p["q_norm"], scale=hd ** -0.5 * math.log2(math.e),
                         transposed_tile=tq, side_jobs=side_q)
        keep(names_q, cast)
        (v32, vt), cast = _proj(h, wts["v"], attn_v, "v", transposed_tile=tk, side_jobs=side_v)
        keep(names_v, cast)
        attn = _attn_prompt(qt, k3, vt, p["lams"], p["subln"], n_heads, lam_i, tq, tk)
    else:
        cache_pool, cache_k, cache_v = cache
        pos0 = cache_k.shape[1]
        hist = cache_pool
        q, cast = _proj(h, wts["q"], attn_qk, "q", gain=p["q_norm"], scale=hd ** -0.5, side_jobs=side_q)
        keep(names_q, cast)
        (v32, v16), cast = _proj(h, wts["v"], attn_v, "v", side_jobs=side_v)
        keep(names_v, cast)
        attn = _attn_cached(q.reshape(b, t, attn_qk), cache_k.reshape(b, pos0, attn_qk), k3,
                            cache_v.reshape(b, pos0, attn_v), v16.reshape(b, t, attn_v),
                            p["lams"], p["subln"], n_heads, lam_i)
    hist0 = jnp.pad(hist, ((0, 0), (HIST_ROWS - POOL_HIST, 0), (0, 0)))
    mix = _poolmix(u3, hist0, wts["pool"], p["pool_scale"], gates.reshape(b, t, 2 * d), attn, pos0)

    x1 = _mm_res(mix.reshape(b * t, d), wts["out"], x2)
    h2 = _rmsnorm(x1, p["norm_ffn"])
    names, side = jobs(("ffn_out", p["w_ffn_out"], 0, d))
    act, cast = _swiglu(h2, wts["ffn_gate"], wts["ffn_up"], side_jobs=side)
    keep(names, cast)
    y = _mm_res(act, wts["ffn_out"], x1)

    new_pool = jnp.concatenate([hist, u3], axis=1)[:, -POOL_HIST:] if t < POOL_HIST else u3[:, t - POOL_HIST:]
    new_k = k32.reshape(b, t, n_heads, 2, hd)
    new_v = v32.reshape(b, t, n_heads, dv)
    return y.reshape(b, t, d), new_pool, new_k, new_v


def kernel(x_prompt, x_sample, cache_pool, cache_k, cache_v, norm_mix, w_in, w_pool, pool_scale, q_norm, k_norm,
           lambda_q1, lambda_k1, lambda_q2, lambda_k2, subln, w_out, norm_ffn, w_ffn_in, w_ffn_out):
    depth = w_in.shape[0]
    n_heads, hd, dv = cache_k.shape[3], cache_k.shape[5], cache_v.shape[4]
    xp, xs = x_prompt, x_sample
    outs_p, outs_s = [], []
    for l in range(depth):
        vec = lambda a: a[l].reshape(1, -1).astype(F32)
        p = dict(
            n_heads=n_heads, hd=hd, dv=dv,
            norm_mix=norm_mix[l], norm_ffn=norm_ffn[l], q_norm=q_norm[l], k_norm=k_norm[l],
            pool_scale=pool_scale[l], subln=vec(subln),
            lams=(vec(lambda_q1), vec(lambda_k1), vec(lambda_q2), vec(lambda_k2)),
            w_in=w_in[l], w_pool=w_pool[l], w_out=w_out[l], w_ffn_in=w_ffn_in[l], w_ffn_out=w_ffn_out[l],
        )
        wts = {}
        xp, pool_p, k_p, v_p = _layer(xp, None, l, p, wts)
        xs, pool_s, k_s, v_s = _layer(xs, (cache_pool[l], cache_k[l], cache_v[l]), l, p, wts)
        outs_p.append((pool_p, k_p, v_p))
        outs_s.append((pool_s, k_s, v_s))
    stack = lambda outs, i: outs[0][i][None] if depth == 1 else jnp.stack([o[i] for o in outs])
    return (xp, xs, stack(outs_p, 0), stack(outs_p, 1), stack(outs_p, 2),
            stack(outs_s, 0), stack(outs_s, 1), stack(outs_s, 2))
```

```python
import functools
import math

import jax
import jax.numpy as jnp
from jax import lax
from jax.experimental import pallas as pl
from jax.experimental.pallas import tpu as pltpu

F32 = jnp.float32
BF16 = jnp.bfloat16

CHUNK = 64
POOL_WINDOWS = (2, 4, 8, 16)
POOL_HIST = max(POOL_WINDOWS) - 1
HIST_ROWS = 16
EPS = 1e-6
LANES = 128
BF16_SUBLANES = 16

V7X_VMEM_BYTES = 64 * 1024 * 1024
VMEM_LIMIT = V7X_VMEM_BYTES - 6 * 1024 * 1024


def _params(n_grid_axes):
    return pltpu.CompilerParams(dimension_semantics=("arbitrary",) * n_grid_axes,
                                vmem_limit_bytes=VMEM_LIMIT)


def _tile(n, candidates):
    for c in candidates:
        if n % c == 0:
            return c
    return n


def _lambda_init(layer):
    return 0.8 - 0.6 * math.exp(-0.3 * layer)


def _with_side_casts(body, n_in, n_side):
    def side_cast_kernel(*refs):
        ins, srcs, rest = refs[:n_in], refs[n_in:n_in + n_side], refs[n_in + n_side:]
        outs, dsts = rest[:len(rest) - n_side], rest[len(rest) - n_side:]
        body(*ins, *outs)
        for src, dst in zip(srcs, dsts):
            dst[...] = src[...].astype(dst.dtype)
    return side_cast_kernel


def _side_casts(jobs, grid):
    n_steps = math.prod(grid)

    def step_of(*g):
        step = g[0]
        for extent, idx in zip(grid[1:], g[1:]):
            step = step * extent + idx
        return step

    srcs, in_specs, out_specs, out_shapes, riding = [], [], [], [], []
    for src, col0, ncols in jobs:
        r = src.shape[0]
        wb = math.gcd(col0, ncols) if col0 else ncols
        ncb = ncols // wb
        rows = r * ncb // n_steps
        fits = (n_steps % ncb == 0 and r % (n_steps // ncb) == 0
                and rows % BF16_SUBLANES == 0 and wb % LANES == 0)
        riding.append(fits)
        if not fits:
            continue
        cb0 = col0 // wb
        srcs.append(src)
        in_specs.append(pl.BlockSpec(
            (rows, wb), lambda *g, ncb=ncb, cb0=cb0: (step_of(*g) // ncb, cb0 + step_of(*g) % ncb)))
        out_specs.append(pl.BlockSpec((rows, wb), lambda *g, ncb=ncb: (step_of(*g) // ncb, step_of(*g) % ncb)))
        out_shapes.append(jax.ShapeDtypeStruct((r, ncols), BF16))

    def finish(results):
        results = list(results)
        return [results.pop(0) if fits else src[:, col0:col0 + ncols].astype(BF16)
                for fits, (src, col0, ncols) in zip(riding, jobs)]

    return srcs, in_specs, out_specs, out_shapes, finish


def _rmsnorm_kernel(x_ref, g_ref, o_ref):
    x = x_ref[...]
    ms = jnp.mean(x * x, axis=-1, keepdims=True)
    o_ref[...] = (x * lax.rsqrt(ms + EPS) * g_ref[...]).astype(o_ref.dtype)


def _rmsnorm(x, g):
    m, d = x.shape
    tm = _tile(m, (256, 128, 64, 32, 16))
    return pl.pallas_call(
        _rmsnorm_kernel,
        grid=(m // tm,),
        in_specs=[pl.BlockSpec((tm, d), lambda i: (i, 0)),
                  pl.BlockSpec((1, d), lambda i: (0, 0))],
        out_specs=pl.BlockSpec((tm, d), lambda i: (i, 0)),
        out_shape=jax.ShapeDtypeStruct((m, d), BF16),
        compiler_params=_params(1),
        name="rmsnorm",
    )(x, g.reshape(1, d).astype(F32))


def _proj_kernel(a_ref, w_ref, g_ref, *out_refs, mode, scale, transposed):
    z = jnp.dot(a_ref[...], w_ref[...], preferred_element_type=F32)
    if mode == "u":
        out_refs[0][...] = z
    elif mode == "v":
        out_refs[0][...] = z
        if transposed:
            rows = out_refs[1].shape[-1]
            for r in range(out_refs[1].shape[0]):
                out_refs[1][r] = z[r * rows:(r + 1) * rows, :].T.astype(BF16)
        else:
            out_refs[1][...] = z.astype(BF16)
    elif mode == "gate":
        out_refs[0][...] = jax.nn.sigmoid(z).astype(BF16)
    else:
        g = g_ref[...]
        hd = g.shape[-1]
        for c in range(z.shape[1] // hd):
            sl = slice(c * hd, (c + 1) * hd)
            zc = z[:, sl]
            y = zc * lax.rsqrt(jnp.mean(zc * zc, axis=-1, keepdims=True) + EPS) * g
            if mode == "k":
                out_refs[0][:, c, :] = y
                out_refs[1][:, sl] = y.astype(BF16)
            elif transposed:
                out_refs[0][sl, :] = (y * scale).T.astype(BF16)
            else:
                out_refs[0][:, sl] = (y * scale).astype(BF16)


def _proj(h, w, n, mode, gain=None, scale=1.0, transposed_tile=0, side_jobs=()):
    m, k = h.shape
    two_out = mode in ("k", "v")
    transposed = transposed_tile > 0
    tm = _tile(m, (512,) if mode == "k" else (1024, 512, 256, 128, 64, 32, 16))
    tn = _tile(n, (512,) if mode == "v" else (1024, 512, 256, 128))
    slabs = tm // transposed_tile if transposed else 1
    if gain is None:
        gain = jnp.ones((1, 128), F32)
    gain = gain.reshape(1, -1).astype(F32)
    o_spec = pl.BlockSpec((tm, tn), lambda i, j: (i, j))
    if mode == "u":
        out_shape, out_specs = [jax.ShapeDtypeStruct((m, n), F32)], [o_spec]
    elif mode == "v" and transposed:
        out_shape = [jax.ShapeDtypeStruct((m, n), F32),
                     jax.ShapeDtypeStruct((m // transposed_tile, n, transposed_tile), BF16)]
        out_specs = [o_spec, pl.BlockSpec((slabs, tn, transposed_tile), lambda i, j: (i, j, 0))]
    elif mode == "k":
        hd = gain.shape[-1]
        out_shape = [jax.ShapeDtypeStruct((m, n // hd, hd), F32), jax.ShapeDtypeStruct((m, n), BF16)]
        out_specs = [pl.BlockSpec((tm, tn // hd, hd), lambda i, j: (i, j, 0)), o_spec]
    elif two_out:
        out_shape = [jax.ShapeDtypeStruct((m, n), F32), jax.ShapeDtypeStruct((m, n), BF16)]
        out_specs = [o_spec, o_spec]
    elif transposed:
        out_shape = [jax.ShapeDtypeStruct((n, m), BF16)]
        out_specs = [pl.BlockSpec((tn, tm), lambda i, j: (j, i))]
    else:
        out_shape, out_specs = [jax.ShapeDtypeStruct((m, n), BF16)], [o_spec]
    grid = (m // tm, n // tn)
    srcs, side_in, side_out, side_shapes, finish = _side_casts(side_jobs, grid)
    n_main = len(out_shape)
    outs = pl.pallas_call(
        _with_side_casts(functools.partial(_proj_kernel, mode=mode, scale=scale, transposed=transposed),
                         3, len(srcs)),
        grid=grid,
        in_specs=[pl.BlockSpec((tm, k), lambda i, j: (i, 0)),
                  pl.BlockSpec((k, tn), lambda i, j: (0, j)),
                  pl.BlockSpec(gain.shape, lambda i, j: (0, 0))] + side_in,
        out_specs=out_specs + side_out,
        out_shape=out_shape + side_shapes,
        compiler_params=_params(2),
        name="proj_" + mode,
    )(h, w, gain, *srcs)
    main = tuple(outs[:n_main]) if two_out else outs[0]
    return main, finish(outs[n_main:])


def _lambda(lq1, lk1, lq2, lk2, lam_i):
    a = jnp.sum(lq1[...] * lk1[...], axis=-1, keepdims=True)
    b = jnp.sum(lq2[...] * lk2[...], axis=-1, keepdims=True)
    return jnp.exp(a) - jnp.exp(b) + lam_i


def _head_out(o0, o1, lam, sub, lam_i):
    o = o0 - lam * o1
    y = o * lax.rsqrt(jnp.mean(o * o, axis=-1, keepdims=True) + EPS) * sub
    return y * (1.0 - lam_i)


def _nt_dot(a, b):
    return lax.dot_general(a, b, (((1,), (1,)), ((), ())), preferred_element_type=F32)


def _attn_prompt_kernel(qt_ref, k_ref, vt_ref, lq1, lk1, lq2, lk2, sub_ref, o_ref, acc_sc, sa_sc, sb_sc,
                        *, tq, tk, hd, lam_i):
    qi = pl.program_id(2)
    q1t = qt_ref[:hd, :]
    q2t = qt_ref[hd:, :]
    acc_sc[...] = jnp.zeros(acc_sc.shape, F32)

    def scores(j):
        k = k_ref[0, pl.ds(pl.multiple_of(j * tk, tk), tk), :]
        return jnp.concatenate([jnp.dot(k[:, :hd], q1t, preferred_element_type=F32),
                                jnp.dot(k[:, hd:], q2t, preferred_element_type=F32)], axis=1)

    def update(j, s, m_old, l_old):
        m_new = jnp.maximum(m_old, jnp.max(s, axis=0, keepdims=True))
        alpha = jnp.exp2(m_old - m_new)
        p = jnp.exp2(s - m_new)
        l_new = alpha * l_old + jnp.sum(p, axis=0, keepdims=True)
        acc_sc[...] = alpha * acc_sc[...] + jnp.dot(vt_ref[j], p.astype(BF16), preferred_element_type=F32)
        return m_new, l_new

    def put_scores(j, buf):
        s = scores(j)
        buf[...] = s

    def step(j, cur, nxt, carry):
        put_scores(j + 1, nxt)
        return update(j, cur[...], *carry)

    n_full = qi
    odd = n_full % 2

    @pl.when(odd == 0)
    def _():
        put_scores(0, sa_sc)

    @pl.when(odd == 1)
    def _():
        put_scores(0, sb_sc)

    init = (jnp.full((1, 2 * tq), -jnp.inf, F32), jnp.zeros((1, 2 * tq), F32))
    carry = lax.fori_loop(0, odd, lambda _, c: step(0, sb_sc, sa_sc, c), init)

    def pair(jj, c):
        j = odd + 2 * jj
        return step(j + 1, sb_sc, sa_sc, step(j, sa_sc, sb_sc, c))

    m, l = lax.fori_loop(0, n_full // 2, pair, carry)

    q_chunk = lax.broadcasted_iota(jnp.int32, (tk, tq), 1) // CHUNK
    k_chunk = lax.broadcasted_iota(jnp.int32, (tk, tq), 0) // CHUNK
    mask = k_chunk <= q_chunk
    s = jnp.where(jnp.concatenate([mask, mask], axis=1), sa_sc[...], jnp.finfo(F32).min)
    m, l = update(n_full, s, m, l)

    inv_l = 1.0 / l
    acc = acc_sc[...]
    lam = _lambda(lq1, lk1, lq2, lk2, lam_i)
    o0 = (acc[:, :tq] * inv_l[:, :tq]).T
    o1 = (acc[:, tq:] * inv_l[:, tq:]).T
    o_ref[0] = _head_out(o0, o1, lam, sub_ref[...], lam_i).astype(o_ref.dtype)


def _attn_prompt(qt, k, vt, lams, subln, n_heads, lam_i, tq, tk):
    b, t, qcols = k.shape
    hd = qcols // (2 * n_heads)
    dv = vt.shape[1] // n_heads
    nq = t // tq
    vec = pl.BlockSpec((1, hd), lambda bi, h, i: (0, 0))
    return pl.pallas_call(
        functools.partial(_attn_prompt_kernel, tq=tq, tk=tk, hd=hd, lam_i=lam_i),
        grid=(b, n_heads, nq),
        in_specs=[pl.BlockSpec((2 * hd, tq), lambda bi, h, i: (h, bi * nq + i)),
                  pl.BlockSpec((1, t, 2 * hd), lambda bi, h, i: (bi, 0, h)),
                  pl.BlockSpec((t // tk, dv, tk), lambda bi, h, i: (bi, h, 0)),
                  vec, vec, vec, vec,
                  pl.BlockSpec((1, dv), lambda bi, h, i: (0, 0))],
        out_specs=pl.BlockSpec((1, tq, dv), lambda bi, h, i: (bi, i, h)),
        out_shape=jax.ShapeDtypeStruct((b, t, n_heads * dv), BF16),
        scratch_shapes=[pltpu.VMEM((dv, 2 * tq), F32), pltpu.VMEM((tk, 2 * tq), F32),
                        pltpu.VMEM((tk, 2 * tq), F32)],
        compiler_params=_params(3),
        name="attn_prompt",
    )(qt, k, vt, *lams, subln)


def _attn_cached_kernel(q_ref, kc_ref, kn_ref, vc_ref, vn_ref, lq1, lk1, lq2, lk2, sub_ref, o_ref,
                        *, hd, dv, lam_i):
    ts = q_ref.shape[1]
    past = kc_ref.shape[1]
    q_chunk = (past + lax.broadcasted_iota(jnp.int32, (ts, 1), 0)) // CHUNK
    mask_c = lax.broadcasted_iota(jnp.int32, (ts, past), 1) // CHUNK <= q_chunk
    mask_n = (past + lax.broadcasted_iota(jnp.int32, (ts, ts), 1)) // CHUNK <= q_chunk
    neg = jnp.finfo(F32).min
    lam = _lambda(lq1, lk1, lq2, lk2, lam_i)
    for h in range(vc_ref.shape[2]):
        v_cols = slice(h * dv, (h + 1) * dv)
        v_c = vc_ref[0, :, h, :].astype(BF16)
        v_n = vn_ref[0, :, v_cols]
        outs = []
        for half in range(2):
            g = 2 * h + half
            cols = slice(g * hd, (g + 1) * hd)
            q = q_ref[0, :, cols]
            sc = jnp.where(mask_c, _nt_dot(q, kc_ref[0, :, g, :].astype(BF16)), neg)
            sn = jnp.where(mask_n, _nt_dot(q, kn_ref[0, :, cols]), neg)
            m = jnp.maximum(jnp.max(sc, axis=-1, keepdims=True), jnp.max(sn, axis=-1, keepdims=True))
            pc = jnp.exp(sc - m)
            pn = jnp.exp(sn - m)
            l = jnp.sum(pc, axis=-1, keepdims=True) + jnp.sum(pn, axis=-1, keepdims=True)
            o = (jnp.dot(pc.astype(BF16), v_c, preferred_element_type=F32)
                 + jnp.dot(pn.astype(BF16), v_n, preferred_element_type=F32))
            outs.append(o * (1.0 / l))
        o_ref[0, :, v_cols] = _head_out(outs[0], outs[1], lam, sub_ref[...], lam_i).astype(o_ref.dtype)


def _attn_cached(q, k_cache, k_new, v_cache, v_new, lams, subln, lam_i):
    b, ts, _ = q.shape
    _, past, n_heads, dv = v_cache.shape
    hd = k_cache.shape[-1]
    hb = _tile(n_heads, (8,))
    vec = pl.BlockSpec((1, hd), lambda bi, h: (0, 0))
    return pl.pallas_call(
        functools.partial(_attn_cached_kernel, hd=hd, dv=dv, lam_i=lam_i),
        grid=(b, n_heads // hb),
        in_specs=[pl.BlockSpec((1, ts, hb * 2 * hd), lambda bi, h: (bi, 0, h)),
                  pl.BlockSpec((1, past, 2 * hb, hd), lambda bi, h: (bi, 0, h, 0)),
                  pl.BlockSpec((1, ts, hb * 2 * hd), lambda bi, h: (bi, 0, h)),
                  pl.BlockSpec((1, past, hb, dv), lambda bi, h: (bi, 0, h, 0)),
                  pl.BlockSpec((1, ts, hb * dv), lambda bi, h: (bi, 0, h)),
                  vec, vec, vec, vec,
                  pl.BlockSpec((1, dv), lambda bi, h: (0, 0))],
        out_specs=pl.BlockSpec((1, ts, hb * dv), lambda bi, h: (bi, 0, h)),
        out_shape=jax.ShapeDtypeStruct((b, ts, n_heads * dv), BF16),
        compiler_params=_params(2),
        name="attn_cached",
    )(q, k_cache, k_new, v_cache, v_new, *lams, subln)


def _poolmix_kernel(u_ref, uprev_ref, hist0_ref, wp_ref, ps_ref, ga_ref, gb_ref, at_ref, o_ref, ext_sc,
                    *, tm, pos0):
    i = pl.program_id(1)
    hist = jnp.where(i == 0, hist0_ref[0], uprev_ref[0])
    ext_sc[0:HIST_ROWS, :] = hist
    ext_sc[HIST_ROWS:HIST_ROWS + tm, :] = u_ref[0]
    pos = pos0 + i * tm + lax.broadcasted_iota(jnp.int32, (tm, 1), 0)
    gin = wp_ref.shape[1]
    gout = wp_ref.shape[2]
    for g, w in enumerate(POOL_WINDOWS):
        cin = slice(g * gin, (g + 1) * gin)
        cout = slice(g * gout, (g + 1) * gout)
        u_new = ext_sc[HIST_ROWS:HIST_ROWS + tm, cin]
        win = u_new
        for j in range(1, w):
            win = win + ext_sc[HIST_ROWS - j:HIST_ROWS - j + tm, cin]
        inv_cnt = 1.0 / jnp.minimum(pos + 1, w).astype(F32)
        pooled = win * inv_cnt - u_new
        y = jnp.dot(pooled.astype(BF16), wp_ref[g], preferred_element_type=F32) * ps_ref[:, cout]
        mix = ga_ref[0, :, cout].astype(F32) * y + gb_ref[0, :, cout].astype(F32) * at_ref[0, :, cout].astype(F32)
        o_ref[0, :, cout] = mix.astype(o_ref.dtype)


def _poolmix(u, hist0, w_pool, pool_scale, gates, attn, pos0):
    b, t, pin = u.shape
    d = attn.shape[-1]
    tm = _tile(t, (256, 128, 64, 32, 16))
    hb = tm // HIST_ROWS
    return pl.pallas_call(
        functools.partial(_poolmix_kernel, tm=tm, pos0=pos0),
        grid=(b, t // tm),
        in_specs=[pl.BlockSpec((1, tm, pin), lambda bi, i: (bi, i, 0)),
                  pl.BlockSpec((1, HIST_ROWS, pin), lambda bi, i: (bi, jnp.maximum(i * hb - 1, 0), 0)),
                  pl.BlockSpec((1, HIST_ROWS, pin), lambda bi, i: (bi, 0, 0)),
                  pl.BlockSpec(w_pool.shape, lambda bi, i: (0, 0, 0)),
                  pl.BlockSpec((1, d), lambda bi, i: (0, 0)),
                  pl.BlockSpec((1, tm, d), lambda bi, i: (bi, i, 0)),
                  pl.BlockSpec((1, tm, d), lambda bi, i: (bi, i, 1)),
                  pl.BlockSpec((1, tm, d), lambda bi, i: (bi, i, 0))],
        out_specs=pl.BlockSpec((1, tm, d), lambda bi, i: (bi, i, 0)),
        out_shape=jax.ShapeDtypeStruct((b, t, d), BF16),
        scratch_shapes=[pltpu.VMEM((HIST_ROWS + tm, pin), F32)],
        compiler_params=_params(2),
        name="poolmix",
    )(u, u, hist0, w_pool, pool_scale.reshape(1, d).astype(F32), gates, gates, attn)


def _mm_res_kernel(a_ref, w_ref, r_ref, o_ref):
    o_ref[...] = r_ref[...] + jnp.dot(a_ref[...], w_ref[...], preferred_element_type=F32)


def _mm_res(a, w, res):
    m, k = a.shape
    n = w.shape[1]
    big_k = k > 8192
    tm = _tile(m, (512,) if big_k else (1024, 512, 256, 128, 64, 32, 16))
    tn = _tile(n, (512, 256) if big_k else (1024, 512, 256, 128))
    return pl.pallas_call(
        _mm_res_kernel,
        grid=(m // tm, n // tn),
        in_specs=[pl.BlockSpec((tm, k), lambda i, j: (i, 0)),
                  pl.BlockSpec((k, tn), lambda i, j: (0, j)),
                  pl.BlockSpec((tm, tn), lambda i, j: (i, j))],
        out_specs=pl.BlockSpec((tm, tn), lambda i, j: (i, j)),
        out_shape=jax.ShapeDtypeStruct((m, n), F32),
        compiler_params=_params(2),
        name="mm_res",
    )(a, w, res)


def _swiglu_kernel(a_ref, wg_ref, wu_ref, o_ref):
    a = a_ref[...]
    gate = jnp.dot(a, wg_ref[...], preferred_element_type=F32)
    up = jnp.dot(a, wu_ref[...], preferred_element_type=F32)
    o_ref[...] = (gate * jax.nn.sigmoid(gate) * up).astype(o_ref.dtype)


def _swiglu(h, wg, wu, side_jobs=()):
    m, k = h.shape
    d_ff = wg.shape[1]
    tm = _tile(m, (2048, 1024, 512, 256, 128, 64, 32, 16))
    tn = _tile(d_ff, (256, 128))
    grid = (m // tm, d_ff // tn)
    srcs, side_in, side_out, side_shapes, finish = _side_casts(side_jobs, grid)
    w_spec = pl.BlockSpec((k, tn), lambda i, j: (0, j))
    outs = pl.pallas_call(
        _with_side_casts(_swiglu_kernel, 3, len(srcs)),
        grid=grid,
        in_specs=[pl.BlockSpec((tm, k), lambda i, j: (i, 0)), w_spec, w_spec] + side_in,
        out_specs=[pl.BlockSpec((tm, tn), lambda i, j: (i, j))] + side_out,
        out_shape=[jax.ShapeDtypeStruct((m, d_ff), BF16)] + side_shapes,
        compiler_params=_params(2),
        name="swiglu",
    )(h, wg, wu, *srcs)
    return outs[0], finish(outs[1:])


def _layer(x, cache, layer, p, wts):
    b, t, d = x.shape
    n_heads, hd, dv = p["n_heads"], p["hd"], p["dv"]
    pool_in = p["w_pool"].shape[0] * p["w_pool"].shape[1]
    attn_qk = n_heads * 2 * hd
    attn_v = n_heads * dv
    d_ff = p["w_ffn_out"].shape[0]
    o1 = pool_in
    o2 = o1 + attn_qk
    o3 = o2 + attn_qk
    o4 = o3 + attn_v
    lam_i = _lambda_init(layer)
    x2 = x.reshape(b * t, d)
    w_in, w_ffn_in = p["w_in"], p["w_ffn_in"]

    def jobs(*specs):
        todo = [sp for sp in specs if sp[0] not in wts]
        return [sp[0] for sp in todo], [sp[1:] for sp in todo]

    def keep(names, cast):
        wts.update(zip(names, cast))

    if "u" not in wts:
        wts["u"] = w_in[:, :o1].astype(BF16)
        wts["pool"] = p["w_pool"].astype(BF16)
    h = _rmsnorm(x2, p["norm_mix"])
    names, side = jobs(("k", w_in, o2, attn_qk))
    u, cast = _proj(h, wts["u"], pool_in, "u", side_jobs=side)
    keep(names, cast)
    names, side = jobs(("gate", w_in, o4, 2 * d))
    (k32, k16), cast = _proj(h, wts["k"], attn_qk, "k", gain=p["k_norm"], side_jobs=side)
    keep(names, cast)
    names, side = jobs(("q", w_in, o1, attn_qk), ("v", w_in, o3, attn_v), ("out", p["w_out"], 0, d))
    gates, cast = _proj(h, wts["gate"], 2 * d, "gate", side_jobs=side)
    keep(names, cast)
    u3 = u.reshape(b, t, pool_in)
    k3 = k16.reshape(b, t, attn_qk)
    names_q, side_q = jobs(("ffn_gate", w_ffn_in, 0, d_ff))
    names_v, side_v = jobs(("ffn_up", w_ffn_in, d_ff, d_ff))
    if cache is None:
        pos0 = 0
        hist = jnp.zeros((b, POOL_HIST, pool_in), F32)
        tq = _tile(t, (512, 256, 128, 64))
        tk = tq
        qt, cast = _proj(h, wts["q"], attn_qk, "q", gain=p["q_norm"], scale=hd ** -0.5 * math.log2(math.e),
                         transposed_tile=tq, side_jobs=side_q)
        keep(names_q, cast)
        (v32, vt), cast = _proj(h, wts["v"], attn_v, "v", transposed_tile=tk, side_jobs=side_v)
        keep(names_v, cast)
        attn = _attn_prompt(qt, k3, vt, p["lams"], p["subln"], n_heads, lam_i, tq, tk)
    else:
        cache_pool, cache_k, cache_v = cache
        pos0 = cache_k.shape[1]
        hist = cache_pool
        q, cast = _proj(h, wts["q"], attn_qk, "q", gain=p["q_norm"], scale=hd ** -0.5, side_jobs=side_q)
        keep(names_q, cast)
        (v32, v16), cast = _proj(h, wts["v"], attn_v, "v", side_jobs=side_v)
        keep(names_v, cast)
        attn = _attn_cached(q.reshape(b, t, attn_qk), cache_k.reshape(b, pos0, 2 * n_heads, hd), k3,
                            cache_v, v16.reshape(b, t, attn_v), p["lams"], p["subln"], lam_i)
    hist0 = jnp.pad(hist, ((0, 0), (HIST_ROWS - POOL_HIST, 0), (0, 0)))
    mix = _poolmix(u3, hist0, wts["pool"], p["pool_scale"], gates.reshape(b, t, 2 * d), attn, pos0)

    x1 = _mm_res(mix.reshape(b * t, d), wts["out"], x2)
    h2 = _rmsnorm(x1, p["norm_ffn"])
    names, side = jobs(("ffn_out", p["w_ffn_out"], 0, d))
    act, cast = _swiglu(h2, wts["ffn_gate"], wts["ffn_up"], side_jobs=side)
    keep(names, cast)
    y = _mm_res(act, wts["ffn_out"], x1)

    new_pool = jnp.concatenate([hist, u3], axis=1)[:, -POOL_HIST:] if t < POOL_HIST else u3[:, t - POOL_HIST:]
    new_k = k32.reshape(b, t, n_heads, 2, hd)
    new_v = v32.reshape(b, t, n_heads, dv)
    return y.reshape(b, t, d), new_pool, new_k, new_v


def kernel(x_prompt, x_sample, cache_pool, cache_k, cache_v, norm_mix, w_in, w_pool, pool_scale, q_norm, k_norm,
           lambda_q1, lambda_k1, lambda_q2, lambda_k2, subln, w_out, norm_ffn, w_ffn_in, w_ffn_out):
    depth = w_in.shape[0]
    n_heads, hd, dv = cache_k.shape[3], cache_k.shape[5], cache_v.shape[4]
    xp, xs = x_prompt, x_sample
    outs_p, outs_s = [], []
    for l in range(depth):
        vec = lambda a: a[l].reshape(1, -1).astype(F32)
        p = dict(
            n_heads=n_heads, hd=hd, dv=dv,
            norm_mix=norm_mix[l], norm_ffn=norm_ffn[l], q_norm=q_norm[l], k_norm=k_norm[l],
            pool_scale=pool_scale[l], subln=vec(subln),
            lams=(vec(lambda_q1), vec(lambda_k1), vec(lambda_q2), vec(lambda_k2)),
            w_in=w_in[l], w_pool=w_pool[l], w_out=w_out[l], w_ffn_in=w_ffn_in[l], w_ffn_out=w_ffn_out[l],
        )
        wts = {}
        xp, pool_p, k_p, v_p = _layer(xp, None, l, p, wts)
        xs, pool_s, k_s, v_s = _layer(xs, (cache_pool[l], cache_k[l], cache_v[l]), l, p, wts)
        outs_p.append((pool_p, k_p, v_p))
        outs_s.append((pool_s, k_s, v_s))
    stack = lambda outs, i: outs[0][i][None] if depth == 1 else jnp.stack([o[i] for o in outs])
    return (xp, xs, stack(outs_p, 0), stack(outs_p, 1), stack(outs_p, 2),
            stack(outs_s, 0), stack(outs_s, 1), stack(outs_s, 2))
```

```python
import functools
import math

import jax
import jax.numpy as jnp
from jax import lax
from jax.experimental import pallas as pl
from jax.experimental.pallas import tpu as pltpu

F32 = jnp.float32
BF16 = jnp.bfloat16

CHUNK = 64
POOL_WINDOWS = (2, 4, 8, 16)
POOL_HIST = max(POOL_WINDOWS) - 1
HIST_ROWS = 16
EPS = 1e-6
LANES = 128
BF16_SUBLANES = 16

V7X_VMEM_BYTES = 64 * 1024 * 1024
VMEM_LIMIT = V7X_VMEM_BYTES - 6 * 1024 * 1024


def _params(n_grid_axes):
    return pltpu.CompilerParams(dimension_semantics=("arbitrary",) * n_grid_axes,
                                vmem_limit_bytes=VMEM_LIMIT)


def _tile(n, candidates):
    for c in candidates:
        if n % c == 0:
            return c
    return n


def _lambda_init(layer):
    return 0.8 - 0.6 * math.exp(-0.3 * layer)


def _with_side_casts(body, n_in, n_side):
    def side_cast_kernel(*refs):
        ins, srcs, rest = refs[:n_in], refs[n_in:n_in + n_side], refs[n_in + n_side:]
        outs, dsts = rest[:len(rest) - n_side], rest[len(rest) - n_side:]
        body(*ins, *outs)
        for src, dst in zip(srcs, dsts):
            dst[...] = src[...].astype(dst.dtype)
    return side_cast_kernel


def _side_casts(jobs, grid):
    n_steps = math.prod(grid)

    def step_of(*g):
        step = g[0]
        for extent, idx in zip(grid[1:], g[1:]):
            step = step * extent + idx
        return step

    srcs, in_specs, out_specs, out_shapes, riding = [], [], [], [], []
    for src, col0, ncols in jobs:
        r = src.shape[0]
        wb = math.gcd(col0, ncols) if col0 else ncols
        ncb = ncols // wb
        rows = r * ncb // n_steps
        fits = (n_steps % ncb == 0 and r % (n_steps // ncb) == 0
                and rows % BF16_SUBLANES == 0 and wb % LANES == 0)
        riding.append(fits)
        if not fits:
            continue
        cb0 = col0 // wb
        srcs.append(src)
        in_specs.append(pl.BlockSpec(
            (rows, wb), lambda *g, ncb=ncb, cb0=cb0: (step_of(*g) // ncb, cb0 + step_of(*g) % ncb)))
        out_specs.append(pl.BlockSpec((rows, wb), lambda *g, ncb=ncb: (step_of(*g) // ncb, step_of(*g) % ncb)))
        out_shapes.append(jax.ShapeDtypeStruct((r, ncols), BF16))

    def finish(results):
        results = list(results)
        return [results.pop(0) if fits else src[:, col0:col0 + ncols].astype(BF16)
                for fits, (src, col0, ncols) in zip(riding, jobs)]

    return srcs, in_specs, out_specs, out_shapes, finish


def _rmsnorm_kernel(x_ref, g_ref, o_ref):
    x = x_ref[...]
    ms = jnp.mean(x * x, axis=-1, keepdims=True)
    o_ref[...] = (x * lax.rsqrt(ms + EPS) * g_ref[...]).astype(o_ref.dtype)


def _rmsnorm(x, g):
    m, d = x.shape
    tm = _tile(m, (256, 128, 64, 32, 16))
    return pl.pallas_call(
        _rmsnorm_kernel,
        grid=(m // tm,),
        in_specs=[pl.BlockSpec((tm, d), lambda i: (i, 0)),
                  pl.BlockSpec((1, d), lambda i: (0, 0))],
        out_specs=pl.BlockSpec((tm, d), lambda i: (i, 0)),
        out_shape=jax.ShapeDtypeStruct((m, d), BF16),
        compiler_params=_params(1),
        name="rmsnorm",
    )(x, g.reshape(1, d).astype(F32))


def _proj_kernel(a_ref, w_ref, g_ref, *out_refs, mode, scale, transposed):
    z = jnp.dot(a_ref[...], w_ref[...], preferred_element_type=F32)
    if mode == "u":
        out_refs[0][...] = z
    elif mode == "v":
        out_refs[0][...] = z
        if transposed:
            rows = out_refs[1].shape[-1]
            for r in range(out_refs[1].shape[0]):
                out_refs[1][r] = z[r * rows:(r + 1) * rows, :].T.astype(BF16)
        else:
            out_refs[1][...] = z.astype(BF16)
    elif mode == "gate":
        out_refs[0][...] = jax.nn.sigmoid(z).astype(BF16)
    else:
        g = g_ref[...]
        hd = g.shape[-1]
        for c in range(z.shape[1] // hd):
            sl = slice(c * hd, (c + 1) * hd)
            zc = z[:, sl]
            y = zc * lax.rsqrt(jnp.mean(zc * zc, axis=-1, keepdims=True) + EPS) * g
            if mode == "k":
                out_refs[0][:, c, :] = y
                out_refs[1][:, sl] = y.astype(BF16)
            elif transposed:
                rows = out_refs[0].shape[-1]
                for r in range(out_refs[0].shape[0]):
                    out_refs[0][r, sl, :] = (y[r * rows:(r + 1) * rows, :] * scale).T.astype(BF16)
            else:
                out_refs[0][:, sl] = (y * scale).astype(BF16)


def _proj(h, w, n, mode, gain=None, scale=1.0, transposed_tile=0, side_jobs=()):
    m, k = h.shape
    two_out = mode in ("k", "v")
    transposed = transposed_tile > 0
    tm = _tile(m, (512,) if mode == "k" else (1024, 512, 256, 128, 64, 32, 16))
    tn = _tile(n, (512,) if mode == "v" else (1024, 512, 256, 128))
    slabs = tm // transposed_tile if transposed else 1
    if gain is None:
        gain = jnp.ones((1, 128), F32)
    gain = gain.reshape(1, -1).astype(F32)
    o_spec = pl.BlockSpec((tm, tn), lambda i, j: (i, j))
    if mode == "u":
        out_shape, out_specs = [jax.ShapeDtypeStruct((m, n), F32)], [o_spec]
    elif mode == "v" and transposed:
        out_shape = [jax.ShapeDtypeStruct((m, n), F32),
                     jax.ShapeDtypeStruct((m // transposed_tile, n, transposed_tile), BF16)]
        out_specs = [o_spec, pl.BlockSpec((slabs, tn, transposed_tile), lambda i, j: (i, j, 0))]
    elif mode == "k":
        hd = gain.shape[-1]
        out_shape = [jax.ShapeDtypeStruct((m, n // hd, hd), F32), jax.ShapeDtypeStruct((m, n), BF16)]
        out_specs = [pl.BlockSpec((tm, tn // hd, hd), lambda i, j: (i, j, 0)), o_spec]
    elif two_out:
        out_shape = [jax.ShapeDtypeStruct((m, n), F32), jax.ShapeDtypeStruct((m, n), BF16)]
        out_specs = [o_spec, o_spec]
    elif transposed:
        out_shape = [jax.ShapeDtypeStruct((m // transposed_tile, n, transposed_tile), BF16)]
        out_specs = [pl.BlockSpec((slabs, tn, transposed_tile), lambda i, j: (i, j, 0))]
    else:
        out_shape, out_specs = [jax.ShapeDtypeStruct((m, n), BF16)], [o_spec]
    grid = (m // tm, n // tn)
    srcs, side_in, side_out, side_shapes, finish = _side_casts(side_jobs, grid)
    n_main = len(out_shape)
    outs = pl.pallas_call(
        _with_side_casts(functools.partial(_proj_kernel, mode=mode, scale=scale, transposed=transposed),
                         3, len(srcs)),
        grid=grid,
        in_specs=[pl.BlockSpec((tm, k), lambda i, j: (i, 0)),
                  pl.BlockSpec((k, tn), lambda i, j: (0, j)),
                  pl.BlockSpec(gain.shape, lambda i, j: (0, 0))] + side_in,
        out_specs=out_specs + side_out,
        out_shape=out_shape + side_shapes,
        compiler_params=_params(2),
        name="proj_" + mode,
    )(h, w, gain, *srcs)
    main = tuple(outs[:n_main]) if two_out else outs[0]
    return main, finish(outs[n_main:])


def _lambda(lq1, lk1, lq2, lk2, lam_i):
    a = jnp.sum(lq1[...] * lk1[...], axis=-1, keepdims=True)
    b = jnp.sum(lq2[...] * lk2[...], axis=-1, keepdims=True)
    return jnp.exp(a) - jnp.exp(b) + lam_i


def _head_out(o0, o1, lam, sub, lam_i):
    o = o0 - lam * o1
    y = o * lax.rsqrt(jnp.mean(o * o, axis=-1, keepdims=True) + EPS) * sub
    return y * (1.0 - lam_i)


def _nt_dot(a, b):
    return lax.dot_general(a, b, (((1,), (1,)), ((), ())), preferred_element_type=F32)


def _attn_prompt_kernel(qt_ref, k_ref, vt_ref, lq1, lk1, lq2, lk2, sub_ref, o_ref, acc_sc, sa_sc, sb_sc, sc_sc,
                        *, tq, hd, lam_i):
    nq = qt_ref.shape[0]
    lam = _lambda(lq1, lk1, lq2, lk2, lam_i)

    def put_scores(buf, qi, j):
        q = qt_ref[qi]
        k = k_ref[0, pl.ds(pl.multiple_of(j * tq, tq), tq), :]
        buf[:, :tq] = jnp.dot(k[:, :hd], q[:hd, :], preferred_element_type=F32)
        buf[:, tq:] = jnp.dot(k[:, hd:], q[hd:, :], preferred_element_type=F32)

    def update(j, s, m_old, l_old):
        m_new = jnp.maximum(m_old, jnp.max(s, axis=0, keepdims=True))
        alpha = jnp.exp2(m_old - m_new)
        p = jnp.exp2(s - m_new)
        l_new = alpha * l_old + jnp.sum(p, axis=0, keepdims=True)
        acc_sc[...] = alpha * acc_sc[...] + jnp.dot(vt_ref[j], p.astype(BF16), preferred_element_type=F32)
        return m_new, l_new

    def step(qi, j, cur, nxt, carry):
        put_scores(nxt, qi, j + 1)
        return update(j, cur[...], *carry)

    put_scores(sa_sc, 0, 0)

    def query_tile(qi, _):
        acc_sc[...] = jnp.zeros(acc_sc.shape, F32)
        carry = (jnp.full((1, 2 * tq), -jnp.inf, F32), jnp.zeros((1, 2 * tq), F32))
        peel_two = jnp.logical_and(qi >= 1, qi % 2 == 0).astype(jnp.int32)
        peel_one = (qi % 2 == 1).astype(jnp.int32)
        carry = lax.fori_loop(0, peel_one, lambda _, c: step(qi, 0, sc_sc, sa_sc, c), carry)
        carry = lax.fori_loop(
            0, peel_two, lambda _, c: step(qi, 1, sb_sc, sa_sc, step(qi, 0, sc_sc, sb_sc, c)), carry)
        j0 = 1 + peel_two

        def pair(jj, c):
            j = j0 + 2 * jj
            return step(qi, j + 1, sb_sc, sa_sc, step(qi, j, sa_sc, sb_sc, c))

        m, l = lax.fori_loop(0, (qi - j0) // 2, pair, carry)

        put_scores(sc_sc, jnp.minimum(qi + 1, nq - 1), 0)
        q_chunk = lax.broadcasted_iota(jnp.int32, (tq, tq), 1) // CHUNK
        k_chunk = lax.broadcasted_iota(jnp.int32, (tq, tq), 0) // CHUNK
        mask = k_chunk <= q_chunk
        s = jnp.where(jnp.concatenate([mask, mask], axis=1), sa_sc[...], jnp.finfo(F32).min)
        m, l = update(qi, s, m, l)
        inv_l = 1.0 / l
        acc = acc_sc[...]
        o0 = (acc[:, :tq] * inv_l[:, :tq]).T
        o1 = (acc[:, tq:] * inv_l[:, tq:]).T
        rows = pl.ds(pl.multiple_of(qi * tq, tq), tq)
        o_ref[0, rows, :] = _head_out(o0, o1, lam, sub_ref[...], lam_i).astype(o_ref.dtype)
        return 0

    lax.fori_loop(0, nq, query_tile, 0)


def _attn_prompt(qt, k, vt, lams, subln, n_heads, lam_i):
    b, t, qcols = k.shape
    tq = qt.shape[-1]
    hd = qcols // (2 * n_heads)
    dv = vt.shape[1] // n_heads
    nq = t // tq
    vec = pl.BlockSpec((1, hd), lambda bi, h: (0, 0))
    return pl.pallas_call(
        functools.partial(_attn_prompt_kernel, tq=tq, hd=hd, lam_i=lam_i),
        grid=(b, n_heads),
        in_specs=[pl.BlockSpec((nq, 2 * hd, tq), lambda bi, h: (bi, h, 0)),
                  pl.BlockSpec((1, t, 2 * hd), lambda bi, h: (bi, 0, h)),
                  pl.BlockSpec((nq, dv, tq), lambda bi, h: (bi, h, 0)),
                  vec, vec, vec, vec,
                  pl.BlockSpec((1, dv), lambda bi, h: (0, 0))],
        out_specs=pl.BlockSpec((1, t, dv), lambda bi, h: (bi, 0, h)),
        out_shape=jax.ShapeDtypeStruct((b, t, n_heads * dv), BF16),
        scratch_shapes=[pltpu.VMEM((dv, 2 * tq), F32)] + [pltpu.VMEM((tq, 2 * tq), F32)] * 3,
        compiler_params=_params(2),
        name="attn_prompt",
    )(qt, k, vt, *lams, subln)


def _attn_cached_kernel(q_ref, kc_ref, kn_ref, vc_ref, vn_ref, lq1, lk1, lq2, lk2, sub_ref, o_ref,
                        *, hd, dv, lam_i):
    ts = q_ref.shape[1]
    past = kc_ref.shape[1]
    q_chunk = (past + lax.broadcasted_iota(jnp.int32, (ts, 1), 0)) // CHUNK
    mask_c = lax.broadcasted_iota(jnp.int32, (ts, past), 1) // CHUNK <= q_chunk
    mask_n = (past + lax.broadcasted_iota(jnp.int32, (ts, ts), 1)) // CHUNK <= q_chunk
    neg = jnp.finfo(F32).min
    lam = _lambda(lq1, lk1, lq2, lk2, lam_i)
    for h in range(vc_ref.shape[2]):
        v_cols = slice(h * dv, (h + 1) * dv)
        v_c = vc_ref[0, :, h, :].astype(BF16)
        v_n = vn_ref[0, :, v_cols]
        outs = []
        for half in range(2):
            g = 2 * h + half
            cols = slice(g * hd, (g + 1) * hd)
            q = q_ref[0, :, cols]
            sc = jnp.where(mask_c, _nt_dot(q, kc_ref[0, :, g, :].astype(BF16)), neg)
            sn = jnp.where(mask_n, _nt_dot(q, kn_ref[0, :, cols]), neg)
            m = jnp.maximum(jnp.max(sc, axis=-1, keepdims=True), jnp.max(sn, axis=-1, keepdims=True))
            pc = jnp.exp(sc - m)
            pn = jnp.exp(sn - m)
            l = jnp.sum(pc, axis=-1, keepdims=True) + jnp.sum(pn, axis=-1, keepdims=True)
            o = (jnp.dot(pc.astype(BF16), v_c, preferred_element_type=F32)
                 + jnp.dot(pn.astype(BF16), v_n, preferred_element_type=F32))
            outs.append(o * (1.0 / l))
        o_ref[0, :, v_cols] = _head_out(outs[0], outs[1], lam, sub_ref[...], lam_i).astype(o_ref.dtype)


def _attn_cached(q, k_cache, k_new, v_cache, v_new, lams, subln, lam_i):
    b, ts, _ = q.shape
    _, past, n_heads, dv = v_cache.shape
    hd = k_cache.shape[-1]
    hb = _tile(n_heads, (8,))
    vec = pl.BlockSpec((1, hd), lambda bi, h: (0, 0))
    return pl.pallas_call(
        functools.partial(_attn_cached_kernel, hd=hd, dv=dv, lam_i=lam_i),
        grid=(b, n_heads // hb),
        in_specs=[pl.BlockSpec((1, ts, hb * 2 * hd), lambda bi, h: (bi, 0, h)),
                  pl.BlockSpec((1, past, 2 * hb, hd), lambda bi, h: (bi, 0, h, 0)),
                  pl.BlockSpec((1, ts, hb * 2 * hd), lambda bi, h: (bi, 0, h)),
                  pl.BlockSpec((1, past, hb, dv), lambda bi, h: (bi, 0, h, 0)),
                  pl.BlockSpec((1, ts, hb * dv), lambda bi, h: (bi, 0, h)),
                  vec, vec, vec, vec,
                  pl.BlockSpec((1, dv), lambda bi, h: (0, 0))],
        out_specs=pl.BlockSpec((1, ts, hb * dv), lambda bi, h: (bi, 0, h)),
        out_shape=jax.ShapeDtypeStruct((b, ts, n_heads * dv), BF16),
        compiler_params=_params(2),
        name="attn_cached",
    )(q, k_cache, k_new, v_cache, v_new, *lams, subln)


def _poolmix_kernel(u_ref, uprev_ref, hist0_ref, wp_ref, ps_ref, ga_ref, gb_ref, at_ref, o_ref, ext_sc,
                    *, tm, pos0):
    i = pl.program_id(1)
    hist = jnp.where(i == 0, hist0_ref[0], uprev_ref[0])
    ext_sc[0:HIST_ROWS, :] = hist
    ext_sc[HIST_ROWS:HIST_ROWS + tm, :] = u_ref[0]
    pos = pos0 + i * tm + lax.broadcasted_iota(jnp.int32, (tm, 1), 0)
    gin = wp_ref.shape[1]
    gout = wp_ref.shape[2]
    for g, w in enumerate(POOL_WINDOWS):
        cin = slice(g * gin, (g + 1) * gin)
        cout = slice(g * gout, (g + 1) * gout)
        u_new = ext_sc[HIST_ROWS:HIST_ROWS + tm, cin]
        win = u_new
        for j in range(1, w):
            win = win + ext_sc[HIST_ROWS - j:HIST_ROWS - j + tm, cin]
        inv_cnt = 1.0 / jnp.minimum(pos + 1, w).astype(F32)
        pooled = win * inv_cnt - u_new
        y = jnp.dot(pooled.astype(BF16), wp_ref[g], preferred_element_type=F32) * ps_ref[:, cout]
        mix = ga_ref[0, :, cout].astype(F32) * y + gb_ref[0, :, cout].astype(F32) * at_ref[0, :, cout].astype(F32)
        o_ref[0, :, cout] = mix.astype(o_ref.dtype)


def _poolmix(u, hist0, w_pool, pool_scale, gates, attn, pos0):
    b, t, pin = u.shape
    d = attn.shape[-1]
    tm = _tile(t, (256, 128, 64, 32, 16))
    hb = tm // HIST_ROWS
    return pl.pallas_call(
        functools.partial(_poolmix_kernel, tm=tm, pos0=pos0),
        grid=(b, t // tm),
        in_specs=[pl.BlockSpec((1, tm, pin), lambda bi, i: (bi, i, 0)),
                  pl.BlockSpec((1, HIST_ROWS, pin), lambda bi, i: (bi, jnp.maximum(i * hb - 1, 0), 0)),
                  pl.BlockSpec((1, HIST_ROWS, pin), lambda bi, i: (bi, 0, 0)),
                  pl.BlockSpec(w_pool.shape, lambda bi, i: (0, 0, 0)),
                  pl.BlockSpec((1, d), lambda bi, i: (0, 0)),
                  pl.BlockSpec((1, tm, d), lambda bi, i: (bi, i, 0)),
                  pl.BlockSpec((1, tm, d), lambda bi, i: (bi, i, 1)),
                  pl.BlockSpec((1, tm, d), lambda bi, i: (bi, i, 0))],
        out_specs=pl.BlockSpec((1, tm, d), lambda bi, i: (bi, i, 0)),
        out_shape=jax.ShapeDtypeStruct((b, t, d), BF16),
        scratch_shapes=[pltpu.VMEM((HIST_ROWS + tm, pin), F32)],
        compiler_params=_params(2),
        name="poolmix",
    )(u, u, hist0, w_pool, pool_scale.reshape(1, d).astype(F32), gates, gates, attn)


def _mm_res_kernel(a_ref, w_ref, r_ref, o_ref):
    o_ref[...] = r_ref[...] + jnp.dot(a_ref[...], w_ref[...], preferred_element_type=F32)


def _mm_res(a, w, res):
    m, k = a.shape
    n = w.shape[1]
    big_k = k > 8192
    tm = _tile(m, (512,) if big_k else (1024, 512, 256, 128, 64, 32, 16))
    tn = _tile(n, (512, 256) if big_k else (1024, 512, 256, 128))
    return pl.pallas_call(
        _mm_res_kernel,
        grid=(m // tm, n // tn),
        in_specs=[pl.BlockSpec((tm, k), lambda i, j: (i, 0)),
                  pl.BlockSpec((k, tn), lambda i, j: (0, j)),
                  pl.BlockSpec((tm, tn), lambda i, j: (i, j))],
        out_specs=pl.BlockSpec((tm, tn), lambda i, j: (i, j)),
        out_shape=jax.ShapeDtypeStruct((m, n), F32),
        compiler_params=_params(2),
        name="mm_res",
    )(a, w, res)


def _swiglu_kernel(a_ref, wg_ref, wu_ref, o_ref):
    a = a_ref[...]
    gate = jnp.dot(a, wg_ref[...], preferred_element_type=F32)
    up = jnp.dot(a, wu_ref[...], preferred_element_type=F32)
    o_ref[...] = (gate * jax.nn.sigmoid(gate) * up).astype(o_ref.dtype)


def _swiglu(h, wg, wu, side_jobs=()):
    m, k = h.shape
    d_ff = wg.shape[1]
    tm = _tile(m, (2048, 1024, 512, 256, 128, 64, 32, 16))
    tn = _tile(d_ff, (256, 128))
    grid = (m // tm, d_ff // tn)
    srcs, side_in, side_out, side_shapes, finish = _side_casts(side_jobs, grid)
    w_spec = pl.BlockSpec((k, tn), lambda i, j: (0, j))
    outs = pl.pallas_call(
        _with_side_casts(_swiglu_kernel, 3, len(srcs)),
        grid=grid,
        in_specs=[pl.BlockSpec((tm, k), lambda i, j: (i, 0)), w_spec, w_spec] + side_in,
        out_specs=[pl.BlockSpec((tm, tn), lambda i, j: (i, j))] + side_out,
        out_shape=[jax.ShapeDtypeStruct((m, d_ff), BF16)] + side_shapes,
        compiler_params=_params(2),
        name="swiglu",
    )(h, wg, wu, *srcs)
    return outs[0], finish(outs[1:])


def _layer(x, cache, layer, p, wts):
    b, t, d = x.shape
    n_heads, hd, dv = p["n_heads"], p["hd"], p["dv"]
    pool_in = p["w_pool"].shape[0] * p["w_pool"].shape[1]
    attn_qk = n_heads * 2 * hd
    attn_v = n_heads * dv
    d_ff = p["w_ffn_out"].shape[0]
    o1 = pool_in
    o2 = o1 + attn_qk
    o3 = o2 + attn_qk
    o4 = o3 + attn_v
    lam_i = _lambda_init(layer)
    x2 = x.reshape(b * t, d)
    w_in, w_ffn_in = p["w_in"], p["w_ffn_in"]

    def jobs(*specs):
        todo = [sp for sp in specs if sp[0] not in wts]
        return [sp[0] for sp in todo], [sp[1:] for sp in todo]

    def keep(names, cast):
        wts.update(zip(names, cast))

    if "u" not in wts:
        wts["u"] = w_in[:, :o1].astype(BF16)
        wts["pool"] = p["w_pool"].astype(BF16)
    h = _rmsnorm(x2, p["norm_mix"])
    names, side = jobs(("k", w_in, o2, attn_qk))
    u, cast = _proj(h, wts["u"], pool_in, "u", side_jobs=side)
    keep(names, cast)
    names, side = jobs(("gate", w_in, o4, 2 * d))
    (k32, k16), cast = _proj(h, wts["k"], attn_qk, "k", gain=p["k_norm"], side_jobs=side)
    keep(names, cast)
    names, side = jobs(("q", w_in, o1, attn_qk), ("v", w_in, o3, attn_v), ("out", p["w_out"], 0, d))
    gates, cast = _proj(h, wts["gate"], 2 * d, "gate", side_jobs=side)
    keep(names, cast)
    u3 = u.reshape(b, t, pool_in)
    k3 = k16.reshape(b, t, attn_qk)
    names_q, side_q = jobs(("ffn_gate", w_ffn_in, 0, d_ff))
    names_v, side_v = jobs(("ffn_up", w_ffn_in, d_ff, d_ff))
    if cache is None:
        pos0 = 0
        hist = jnp.zeros((b, POOL_HIST, pool_in), F32)
        tq = _tile(t, (512, 256, 128, 64))
        qt, cast = _proj(h, wts["q"], attn_qk, "q", gain=p["q_norm"], scale=hd ** -0.5 * math.log2(math.e),
                         transposed_tile=tq, side_jobs=side_q)
        keep(names_q, cast)
        (v32, vt), cast = _proj(h, wts["v"], attn_v, "v", transposed_tile=tq, side_jobs=side_v)
        keep(names_v, cast)
        attn = _attn_prompt(qt, k3, vt, p["lams"], p["subln"], n_heads, lam_i)
    else:
        cache_pool, cache_k, cache_v = cache
        pos0 = cache_k.shape[1]
        hist = cache_pool
        q, cast = _proj(h, wts["q"], attn_qk, "q", gain=p["q_norm"], scale=hd ** -0.5, side_jobs=side_q)
        keep(names_q, cast)
        (v32, v16), cast = _proj(h, wts["v"], attn_v, "v", side_jobs=side_v)
        keep(names_v, cast)
        attn = _attn_cached(q.reshape(b, t, attn_qk), cache_k.reshape(b, pos0, 2 * n_heads, hd), k3,
                            cache_v, v16.reshape(b, t, attn_v), p["lams"], p["subln"], lam_i)
    hist0 = jnp.pad(hist, ((0, 0), (HIST_ROWS - POOL_HIST, 0), (0, 0)))
    mix = _poolmix(u3, hist0, wts["pool"], p["pool_scale"], gates.reshape(b, t, 2 * d), attn, pos0)

    x1 = _mm_res(mix.reshape(b * t, d), wts["out"], x2)
    h2 = _rmsnorm(x1, p["norm_ffn"])
    names, side = jobs(("ffn_out", p["w_ffn_out"], 0, d))
    act, cast = _swiglu(h2, wts["ffn_gate"], wts["ffn_up"], side_jobs=side)
    keep(names, cast)
    y = _mm_res(act, wts["ffn_out"], x1)

    new_pool = jnp.concatenate([hist, u3], axis=1)[:, -POOL_HIST:] if t < POOL_HIST else u3[:, t - POOL_HIST:]
    new_k = k32.reshape(b, t, n_heads, 2, hd)
    new_v = v32.reshape(b, t, n_heads, dv)
    return y.reshape(b, t, d), new_pool, new_k, new_v


def kernel(x_prompt, x_sample, cache_pool, cache_k, cache_v, norm_mix, w_in, w_pool, pool_scale, q_norm, k_norm,
           lambda_q1, lambda_k1, lambda_q2, lambda_k2, subln, w_out, norm_ffn, w_ffn_in, w_ffn_out):
    depth = w_in.shape[0]
    n_heads, hd, dv = cache_k.shape[3], cache_k.shape[5], cache_v.shape[4]
    xp, xs = x_prompt, x_sample
    outs_p, outs_s = [], []
    for l in range(depth):
        vec = lambda a: a[l].reshape(1, -1).astype(F32)
        p = dict(
            n_heads=n_heads, hd=hd, dv=dv,
            norm_mix=norm_mix[l], norm_ffn=norm_ffn[l], q_norm=q_norm[l], k_norm=k_norm[l],
            pool_scale=pool_scale[l], subln=vec(subln),
            lams=(vec(lambda_q1), vec(lambda_k1), vec(lambda_q2), vec(lambda_k2)),
            w_in=w_in[l], w_pool=w_pool[l], w_out=w_out[l], w_ffn_in=w_ffn_in[l], w_ffn_out=w_ffn_out[l],
        )
        wts = {}
        xp, pool_p, k_p, v_p = _layer(xp, None, l, p, wts)
        xs, pool_s, k_s, v_s = _layer(xs, (cache_pool[l], cache_k[l], cache_v[l]), l, p, wts)
        outs_p.append((pool_p, k_p, v_p))
        outs_s.append((pool_s, k_s, v_s))
    stack = lambda outs, i: outs[0][i][None] if depth == 1 else jnp.stack([o[i] for o in outs])
    return (xp, xs, stack(outs_p, 0), stack(outs_p, 1), stack(outs_p, 2),
            stack(outs_s, 0), stack(outs_s, 1), stack(outs_s, 2))
```

```python
import functools
import math

import jax
import jax.numpy as jnp
from jax import lax
from jax.experimental import pallas as pl
from jax.experimental.pallas import tpu as pltpu

F32 = jnp.float32
BF16 = jnp.bfloat16

CHUNK = 64
POOL_WINDOWS = (2, 4, 8, 16)
POOL_HIST = max(POOL_WINDOWS) - 1
HIST_ROWS = 16
EPS = 1e-6
LANES = 128
BF16_SUBLANES = 16

V7X_VMEM_BYTES = 64 * 1024 * 1024
VMEM_LIMIT = V7X_VMEM_BYTES - 6 * 1024 * 1024


def _params(n_grid_axes):
    return pltpu.CompilerParams(dimension_semantics=("arbitrary",) * n_grid_axes,
                                vmem_limit_bytes=VMEM_LIMIT)


def _tile(n, candidates):
    for c in candidates:
        if n % c == 0:
            return c
    return n


def _weight_specs(k, tn, n_row_tiles):
    parts = 1
    if n_row_tiles == 1:
        parts = next((p for p in (4, 2) if k % (p * LANES) == 0), 1)
    return [pl.BlockSpec((k // parts, tn), lambda i, j, s=s: (s, j)) for s in range(parts)]


def _slab_dot(a_ref, w_refs):
    kp = a_ref.shape[1] // len(w_refs)
    z = jnp.dot(a_ref[:, :kp], w_refs[0][...], preferred_element_type=F32)
    for s in range(1, len(w_refs)):
        z += jnp.dot(a_ref[:, s * kp:(s + 1) * kp], w_refs[s][...], preferred_element_type=F32)
    return z


def _lambda_init(layer):
    return 0.8 - 0.6 * math.exp(-0.3 * layer)


def _with_side_casts(body, n_in, n_side):
    def side_cast_kernel(*refs):
        ins, srcs, rest = refs[:n_in], refs[n_in:n_in + n_side], refs[n_in + n_side:]
        outs, dsts = rest[:len(rest) - n_side], rest[len(rest) - n_side:]
        body(*ins, *outs)
        for src, dst in zip(srcs, dsts):
            dst[...] = src[...].astype(dst.dtype)
    return side_cast_kernel


def _side_casts(jobs, grid):
    n_steps = math.prod(grid)

    def step_of(*g):
        step = g[0]
        for extent, idx in zip(grid[1:], g[1:]):
            step = step * extent + idx
        return step

    srcs, in_specs, out_specs, out_shapes, riding = [], [], [], [], []
    for src, col0, ncols in jobs:
        r = src.shape[0]
        wb = math.gcd(col0, ncols) if col0 else ncols
        ncb = ncols // wb
        rows = r * ncb // n_steps
        fits = (n_steps % ncb == 0 and r % (n_steps // ncb) == 0
                and rows % BF16_SUBLANES == 0 and wb % LANES == 0)
        riding.append(fits)
        if not fits:
            continue
        cb0 = col0 // wb
        srcs.append(src)
        in_specs.append(pl.BlockSpec(
            (rows, wb), lambda *g, ncb=ncb, cb0=cb0: (step_of(*g) // ncb, cb0 + step_of(*g) % ncb)))
        out_specs.append(pl.BlockSpec((rows, wb), lambda *g, ncb=ncb: (step_of(*g) // ncb, step_of(*g) % ncb)))
        out_shapes.append(jax.ShapeDtypeStruct((r, ncols), BF16))

    def finish(results):
        results = list(results)
        return [results.pop(0) if fits else src[:, col0:col0 + ncols].astype(BF16)
                for fits, (src, col0, ncols) in zip(riding, jobs)]

    return srcs, in_specs, out_specs, out_shapes, finish


def _rmsnorm_kernel(x_ref, g_ref, o_ref):
    x = x_ref[...]
    ms = jnp.mean(x * x, axis=-1, keepdims=True)
    o_ref[...] = (x * lax.rsqrt(ms + EPS) * g_ref[...]).astype(o_ref.dtype)


def _rmsnorm(x, g):
    m, d = x.shape
    tm = _tile(m, (256, 128, 64, 32, 16))
    return pl.pallas_call(
        _rmsnorm_kernel,
        grid=(m // tm,),
        in_specs=[pl.BlockSpec((tm, d), lambda i: (i, 0)),
                  pl.BlockSpec((1, d), lambda i: (0, 0))],
        out_specs=pl.BlockSpec((tm, d), lambda i: (i, 0)),
        out_shape=jax.ShapeDtypeStruct((m, d), BF16),
        compiler_params=_params(1),
        name="rmsnorm",
    )(x, g.reshape(1, d).astype(F32))


def _proj_kernel(a_ref, *refs, n_w, mode, scale, transposed):
    w_refs, g_ref, out_refs = refs[:n_w], refs[n_w], refs[n_w + 1:]
    z = _slab_dot(a_ref, w_refs)
    if mode == "u":
        out_refs[0][...] = z
    elif mode == "v":
        out_refs[0][...] = z
        if transposed:
            rows = out_refs[1].shape[-1]
            for r in range(out_refs[1].shape[0]):
                out_refs[1][r] = z[r * rows:(r + 1) * rows, :].T.astype(BF16)
        else:
            out_refs[1][...] = z.astype(BF16)
    elif mode == "gate":
        out_refs[0][...] = jax.nn.sigmoid(z).astype(BF16)
    else:
        g = g_ref[...]
        hd = g.shape[-1]
        for c in range(z.shape[1] // hd):
            sl = slice(c * hd, (c + 1) * hd)
            zc = z[:, sl]
            y = zc * lax.rsqrt(jnp.mean(zc * zc, axis=-1, keepdims=True) + EPS) * g
            if mode == "k":
                out_refs[0][:, c, :] = y
                out_refs[1][:, sl] = y.astype(BF16)
            elif transposed:
                rows = out_refs[0].shape[-1]
                for r in range(out_refs[0].shape[0]):
                    out_refs[0][r, sl, :] = (y[r * rows:(r + 1) * rows, :] * scale).T.astype(BF16)
            else:
                out_refs[0][:, sl] = (y * scale).astype(BF16)


def _proj(h, w, n, mode, gain=None, scale=1.0, transposed_tile=0, side_jobs=()):
    m, k = h.shape
    two_out = mode in ("k", "v")
    transposed = transposed_tile > 0
    tm = _tile(m, (512,) if mode == "k" else (1024, 512, 256, 128, 64, 32, 16))
    narrow = mode == "v" or (m == tm and mode != "k")
    tn = _tile(n, (512,) if narrow else (1024, 512, 256, 128))
    slabs = tm // transposed_tile if transposed else 1
    w_specs = _weight_specs(k, tn, m // tm)
    if gain is None:
        gain = jnp.ones((1, 128), F32)
    gain = gain.reshape(1, -1).astype(F32)
    o_spec = pl.BlockSpec((tm, tn), lambda i, j: (i, j))
    if mode == "u":
        out_shape, out_specs = [jax.ShapeDtypeStruct((m, n), F32)], [o_spec]
    elif mode == "v" and transposed:
        out_shape = [jax.ShapeDtypeStruct((m, n), F32),
                     jax.ShapeDtypeStruct((m // transposed_tile, n, transposed_tile), BF16)]
        out_specs = [o_spec, pl.BlockSpec((slabs, tn, transposed_tile), lambda i, j: (i, j, 0))]
    elif mode == "k":
        hd = gain.shape[-1]
        out_shape = [jax.ShapeDtypeStruct((m, n // hd, hd), F32), jax.ShapeDtypeStruct((m, n), BF16)]
        out_specs = [pl.BlockSpec((tm, tn // hd, hd), lambda i, j: (i, j, 0)), o_spec]
    elif two_out:
        out_shape = [jax.ShapeDtypeStruct((m, n), F32), jax.ShapeDtypeStruct((m, n), BF16)]
        out_specs = [o_spec, o_spec]
    elif transposed:
        out_shape = [jax.ShapeDtypeStruct((m // transposed_tile, n, transposed_tile), BF16)]
        out_specs = [pl.BlockSpec((slabs, tn, transposed_tile), lambda i, j: (i, j, 0))]
    else:
        out_shape, out_specs = [jax.ShapeDtypeStruct((m, n), BF16)], [o_spec]
    grid = (m // tm, n // tn)
    srcs, side_in, side_out, side_shapes, finish = _side_casts(side_jobs, grid)
    n_main = len(out_shape)
    outs = pl.pallas_call(
        _with_side_casts(functools.partial(_proj_kernel, n_w=len(w_specs), mode=mode, scale=scale,
                                           transposed=transposed),
                         2 + len(w_specs), len(srcs)),
        grid=grid,
        in_specs=[pl.BlockSpec((tm, k), lambda i, j: (i, 0)), *w_specs,
                  pl.BlockSpec(gain.shape, lambda i, j: (0, 0))] + side_in,
        out_specs=out_specs + side_out,
        out_shape=out_shape + side_shapes,
        compiler_params=_params(2),
        name="proj_" + mode,
    )(h, *[w] * len(w_specs), gain, *srcs)
    main = tuple(outs[:n_main]) if two_out else outs[0]
    return main, finish(outs[n_main:])


def _lambda(lq1, lk1, lq2, lk2, lam_i):
    a = jnp.sum(lq1[...] * lk1[...], axis=-1, keepdims=True)
    b = jnp.sum(lq2[...] * lk2[...], axis=-1, keepdims=True)
    return jnp.exp(a) - jnp.exp(b) + lam_i


def _head_out(o0, o1, lam, sub, lam_i):
    o = o0 - lam * o1
    y = o * lax.rsqrt(jnp.mean(o * o, axis=-1, keepdims=True) + EPS) * sub
    return y * (1.0 - lam_i)


def _nt_dot(a, b):
    return lax.dot_general(a, b, (((1,), (1,)), ((), ())), preferred_element_type=F32)


def _attn_prompt_kernel(qt_ref, k_ref, vt_ref, lq1, lk1, lq2, lk2, sub_ref, o_ref, acc_sc, sa_sc, sb_sc, sc_sc,
                        *, tq, hd, lam_i):
    nq = qt_ref.shape[0]
    lam = _lambda(lq1, lk1, lq2, lk2, lam_i)

    def put_scores(buf, qi, j):
        q = qt_ref[qi]
        k = k_ref[0, pl.ds(pl.multiple_of(j * tq, tq), tq), :]
        buf[:, :tq] = jnp.dot(k[:, :hd], q[:hd, :], preferred_element_type=F32)
        buf[:, tq:] = jnp.dot(k[:, hd:], q[hd:, :], preferred_element_type=F32)

    def update(j, s, m_old, l_old):
        m_new = jnp.maximum(m_old, jnp.max(s, axis=0, keepdims=True))
        alpha = jnp.exp2(m_old - m_new)
        p = jnp.exp2(s - m_new)
        l_new = alpha * l_old + jnp.sum(p, axis=0, keepdims=True)
        acc_sc[...] = alpha * acc_sc[...] + jnp.dot(vt_ref[j], p.astype(BF16), preferred_element_type=F32)
        return m_new, l_new

    def step(qi, j, cur, nxt, carry):
        put_scores(nxt, qi, j + 1)
        return update(j, cur[...], *carry)

    put_scores(sa_sc, 0, 0)

    def query_tile(qi, _):
        acc_sc[...] = jnp.zeros(acc_sc.shape, F32)
        carry = (jnp.full((1, 2 * tq), -jnp.inf, F32), jnp.zeros((1, 2 * tq), F32))
        peel_two = jnp.logical_and(qi >= 1, qi % 2 == 0).astype(jnp.int32)
        peel_one = (qi % 2 == 1).astype(jnp.int32)
        carry = lax.fori_loop(0, peel_one, lambda _, c: step(qi, 0, sc_sc, sa_sc, c), carry)
        carry = lax.fori_loop(
            0, peel_two, lambda _, c: step(qi, 1, sb_sc, sa_sc, step(qi, 0, sc_sc, sb_sc, c)), carry)
        j0 = 1 + peel_two

        def pair(jj, c):
            j = j0 + 2 * jj
            return step(qi, j + 1, sb_sc, sa_sc, step(qi, j, sa_sc, sb_sc, c))

        m, l = lax.fori_loop(0, (qi - j0) // 2, pair, carry)

        put_scores(sc_sc, jnp.minimum(qi + 1, nq - 1), 0)
        q_chunk = lax.broadcasted_iota(jnp.int32, (tq, tq), 1) // CHUNK
        k_chunk = lax.broadcasted_iota(jnp.int32, (tq, tq), 0) // CHUNK
        mask = k_chunk <= q_chunk
        s = jnp.where(jnp.concatenate([mask, mask], axis=1), sa_sc[...], jnp.finfo(F32).min)
        m, l = update(qi, s, m, l)
        inv_l = 1.0 / l
        acc = acc_sc[...]
        o0 = (acc[:, :tq] * inv_l[:, :tq]).T
        o1 = (acc[:, tq:] * inv_l[:, tq:]).T
        rows = pl.ds(pl.multiple_of(qi * tq, tq), tq)
        o_ref[0, rows, :] = _head_out(o0, o1, lam, sub_ref[...], lam_i).astype(o_ref.dtype)
        return 0

    lax.fori_loop(0, nq, query_tile, 0)


def _attn_prompt(qt, k, vt, lams, subln, n_heads, lam_i):
    b, t, qcols = k.shape
    tq = qt.shape[-1]
    hd = qcols // (2 * n_heads)
    dv = vt.shape[1] // n_heads
    nq = t // tq
    vec = pl.BlockSpec((1, hd), lambda bi, h: (0, 0))
    return pl.pallas_call(
        functools.partial(_attn_prompt_kernel, tq=tq, hd=hd, lam_i=lam_i),
        grid=(b, n_heads),
        in_specs=[pl.BlockSpec((nq, 2 * hd, tq), lambda bi, h: (bi, h, 0)),
                  pl.BlockSpec((1, t, 2 * hd), lambda bi, h: (bi, 0, h)),
                  pl.BlockSpec((nq, dv, tq), lambda bi, h: (bi, h, 0)),
                  vec, vec, vec, vec,
                  pl.BlockSpec((1, dv), lambda bi, h: (0, 0))],
        out_specs=pl.BlockSpec((1, t, dv), lambda bi, h: (bi, 0, h)),
        out_shape=jax.ShapeDtypeStruct((b, t, n_heads * dv), BF16),
        scratch_shapes=[pltpu.VMEM((dv, 2 * tq), F32)] + [pltpu.VMEM((tq, 2 * tq), F32)] * 3,
        compiler_params=_params(2),
        name="attn_prompt",
    )(qt, k, vt, *lams, subln)


def _attn_cached_kernel(q_ref, kc_ref, kn_ref, vc_ref, vn_ref, lq1, lk1, lq2, lk2, sub_ref, o_ref,
                        *, hd, dv, lam_i):
    ts = q_ref.shape[1]
    past = kc_ref.shape[1]
    q_chunk = (past + lax.broadcasted_iota(jnp.int32, (ts, 1), 0)) // CHUNK
    mask_c = lax.broadcasted_iota(jnp.int32, (ts, past), 1) // CHUNK <= q_chunk
    mask_n = (past + lax.broadcasted_iota(jnp.int32, (ts, ts), 1)) // CHUNK <= q_chunk
    neg = jnp.finfo(F32).min
    lam = _lambda(lq1, lk1, lq2, lk2, lam_i)
    for h in range(vc_ref.shape[2]):
        v_cols = slice(h * dv, (h + 1) * dv)
        v_c = vc_ref[0, :, h, :].astype(BF16)
        v_n = vn_ref[0, :, v_cols]
        outs = []
        for half in range(2):
            g = 2 * h + half
            cols = slice(g * hd, (g + 1) * hd)
            q = q_ref[0, :, cols]
            sc = jnp.where(mask_c, _nt_dot(q, kc_ref[0, :, g, :].astype(BF16)), neg)
            sn = jnp.where(mask_n, _nt_dot(q, kn_ref[0, :, cols]), neg)
            m = jnp.maximum(jnp.max(sc, axis=-1, keepdims=True), jnp.max(sn, axis=-1, keepdims=True))
            pc = jnp.exp(sc - m)
            pn = jnp.exp(sn - m)
            l = jnp.sum(pc, axis=-1, keepdims=True) + jnp.sum(pn, axis=-1, keepdims=True)
            o = (jnp.dot(pc.astype(BF16), v_c, preferred_element_type=F32)
                 + jnp.dot(pn.astype(BF16), v_n, preferred_element_type=F32))
            outs.append(o * (1.0 / l))
        o_ref[0, :, v_cols] = _head_out(outs[0], outs[1], lam, sub_ref[...], lam_i).astype(o_ref.dtype)


def _attn_cached(q, k_cache, k_new, v_cache, v_new, lams, subln, lam_i):
    b, ts, _ = q.shape
    _, past, n_heads, dv = v_cache.shape
    hd = k_cache.shape[-1]
    hb = _tile(n_heads, (8,))
    vec = pl.BlockSpec((1, hd), lambda bi, h: (0, 0))
    return pl.pallas_call(
        functools.partial(_attn_cached_kernel, hd=hd, dv=dv, lam_i=lam_i),
        grid=(b, n_heads // hb),
        in_specs=[pl.BlockSpec((1, ts, hb * 2 * hd), lambda bi, h: (bi, 0, h)),
                  pl.BlockSpec((1, past, 2 * hb, hd), lambda bi, h: (bi, 0, h, 0)),
                  pl.BlockSpec((1, ts, hb * 2 * hd), lambda bi, h: (bi, 0, h)),
                  pl.BlockSpec((1, past, hb, dv), lambda bi, h: (bi, 0, h, 0)),
                  pl.BlockSpec((1, ts, hb * dv), lambda bi, h: (bi, 0, h)),
                  vec, vec, vec, vec,
                  pl.BlockSpec((1, dv), lambda bi, h: (0, 0))],
        out_specs=pl.BlockSpec((1, ts, hb * dv), lambda bi, h: (bi, 0, h)),
        out_shape=jax.ShapeDtypeStruct((b, ts, n_heads * dv), BF16),
        compiler_params=_params(2),
        name="attn_cached",
    )(q, k_cache, k_new, v_cache, v_new, *lams, subln)


def _poolmix_kernel(u_ref, uprev_ref, hist0_ref, wp_ref, ps_ref, ga_ref, gb_ref, at_ref, o_ref, ext_sc,
                    *, tm, pos0):
    i = pl.program_id(1)
    hist = jnp.where(i == 0, hist0_ref[0], uprev_ref[0])
    ext_sc[0:HIST_ROWS, :] = hist
    ext_sc[HIST_ROWS:HIST_ROWS + tm, :] = u_ref[0]
    pos = pos0 + i * tm + lax.broadcasted_iota(jnp.int32, (tm, 1), 0)
    gin = wp_ref.shape[1]
    gout = wp_ref.shape[2]
    for g, w in enumerate(POOL_WINDOWS):
        cin = slice(g * gin, (g + 1) * gin)
        cout = slice(g * gout, (g + 1) * gout)
        u_new = ext_sc[HIST_ROWS:HIST_ROWS + tm, cin]
        win = u_new
        for j in range(1, w):
            win = win + ext_sc[HIST_ROWS - j:HIST_ROWS - j + tm, cin]
        inv_cnt = 1.0 / jnp.minimum(pos + 1, w).astype(F32)
        pooled = win * inv_cnt - u_new
        y = jnp.dot(pooled.astype(BF16), wp_ref[g], preferred_element_type=F32) * ps_ref[:, cout]
        mix = ga_ref[0, :, cout].astype(F32) * y + gb_ref[0, :, cout].astype(F32) * at_ref[0, :, cout].astype(F32)
        o_ref[0, :, cout] = mix.astype(o_ref.dtype)


def _poolmix(u, hist0, w_pool, pool_scale, gates, attn, pos0):
    b, t, pin = u.shape
    d = attn.shape[-1]
    tm = _tile(t, (256, 128, 64, 32, 16))
    hb = tm // HIST_ROWS
    return pl.pallas_call(
        functools.partial(_poolmix_kernel, tm=tm, pos0=pos0),
        grid=(b, t // tm),
        in_specs=[pl.BlockSpec((1, tm, pin), lambda bi, i: (bi, i, 0)),
                  pl.BlockSpec((1, HIST_ROWS, pin), lambda bi, i: (bi, jnp.maximum(i * hb - 1, 0), 0)),
                  pl.BlockSpec((1, HIST_ROWS, pin), lambda bi, i: (bi, 0, 0)),
                  pl.BlockSpec(w_pool.shape, lambda bi, i: (0, 0, 0)),
                  pl.BlockSpec((1, d), lambda bi, i: (0, 0)),
                  pl.BlockSpec((1, tm, d), lambda bi, i: (bi, i, 0)),
                  pl.BlockSpec((1, tm, d), lambda bi, i: (bi, i, 1)),
                  pl.BlockSpec((1, tm, d), lambda bi, i: (bi, i, 0))],
        out_specs=pl.BlockSpec((1, tm, d), lambda bi, i: (bi, i, 0)),
        out_shape=jax.ShapeDtypeStruct((b, t, d), BF16),
        scratch_shapes=[pltpu.VMEM((HIST_ROWS + tm, pin), F32)],
        compiler_params=_params(2),
        name="poolmix",
    )(u, u, hist0, w_pool, pool_scale.reshape(1, d).astype(F32), gates, gates, attn)


def _mm_res_kernel(a_ref, *refs):
    *w_refs, r_ref, o_ref = refs
    o_ref[...] = r_ref[...] + _slab_dot(a_ref, w_refs)


def _mm_res(a, w, res):
    m, k = a.shape
    n = w.shape[1]
    big_k = k > 8192
    tm = _tile(m, (512,) if big_k else (1024, 512, 256, 128, 64, 32, 16))
    tn = _tile(n, (512, 256) if big_k or m == tm else (1024, 512, 256, 128))
    w_specs = _weight_specs(k, tn, m // tm)
    return pl.pallas_call(
        _mm_res_kernel,
        grid=(m // tm, n // tn),
        in_specs=[pl.BlockSpec((tm, k), lambda i, j: (i, 0)), *w_specs,
                  pl.BlockSpec((tm, tn), lambda i, j: (i, j))],
        out_specs=pl.BlockSpec((tm, tn), lambda i, j: (i, j)),
        out_shape=jax.ShapeDtypeStruct((m, n), F32),
        compiler_params=_params(2),
        name="mm_res",
    )(a, *[w] * len(w_specs), res)


def _swiglu_kernel(a_ref, *refs):
    n_w = (len(refs) - 1) // 2
    gate = _slab_dot(a_ref, refs[:n_w])
    up = _slab_dot(a_ref, refs[n_w:2 * n_w])
    o_ref = refs[-1]
    o_ref[...] = (gate * jax.nn.sigmoid(gate) * up).astype(o_ref.dtype)


def _swiglu(h, wg, wu, side_jobs=()):
    m, k = h.shape
    d_ff = wg.shape[1]
    tm = _tile(m, (2048, 1024, 512, 256, 128, 64, 32, 16))
    tn = _tile(d_ff, (256, 128))
    grid = (m // tm, d_ff // tn)
    srcs, side_in, side_out, side_shapes, finish = _side_casts(side_jobs, grid)
    w_specs = _weight_specs(k, tn, m // tm)
    outs = pl.pallas_call(
        _with_side_casts(_swiglu_kernel, 1 + 2 * len(w_specs), len(srcs)),
        grid=grid,
        in_specs=[pl.BlockSpec((tm, k), lambda i, j: (i, 0)), *w_specs, *w_specs] + side_in,
        out_specs=[pl.BlockSpec((tm, tn), lambda i, j: (i, j))] + side_out,
        out_shape=[jax.ShapeDtypeStruct((m, d_ff), BF16)] + side_shapes,
        compiler_params=_params(2),
        name="swiglu",
    )(h, *[wg] * len(w_specs), *[wu] * len(w_specs), *srcs)
    return outs[0], finish(outs[1:])


def _layer(x, cache, layer, p, wts):
    b, t, d = x.shape
    n_heads, hd, dv = p["n_heads"], p["hd"], p["dv"]
    pool_in = p["w_pool"].shape[0] * p["w_pool"].shape[1]
    attn_qk = n_heads * 2 * hd
    attn_v = n_heads * dv
    d_ff = p["w_ffn_out"].shape[0]
    o1 = pool_in
    o2 = o1 + attn_qk
    o3 = o2 + attn_qk
    o4 = o3 + attn_v
    lam_i = _lambda_init(layer)
    x2 = x.reshape(b * t, d)
    w_in, w_ffn_in = p["w_in"], p["w_ffn_in"]

    def jobs(*specs):
        todo = [sp for sp in specs if sp[0] not in wts]
        return [sp[0] for sp in todo], [sp[1:] for sp in todo]

    def keep(names, cast):
        wts.update(zip(names, cast))

    if "u" not in wts:
        wts["u"] = w_in[:, :o1].astype(BF16)
        wts["pool"] = p["w_pool"].astype(BF16)
    h = _rmsnorm(x2, p["norm_mix"])
    names, side = jobs(("k", w_in, o2, attn_qk))
    u, cast = _proj(h, wts["u"], pool_in, "u", side_jobs=side)
    keep(names, cast)
    names, side = jobs(("gate", w_in, o4, 2 * d))
    (k32, k16), cast = _proj(h, wts["k"], attn_qk, "k", gain=p["k_norm"], side_jobs=side)
    keep(names, cast)
    names, side = jobs(("q", w_in, o1, attn_qk), ("v", w_in, o3, attn_v), ("out", p["w_out"], 0, d))
    gates, cast = _proj(h, wts["gate"], 2 * d, "gate", side_jobs=side)
    keep(names, cast)
    u3 = u.reshape(b, t, pool_in)
    k3 = k16.reshape(b, t, attn_qk)
    names_q, side_q = jobs(("ffn_gate", w_ffn_in, 0, d_ff))
    names_v, side_v = jobs(("ffn_up", w_ffn_in, d_ff, d_ff))
    if cache is None:
        pos0 = 0
        hist = jnp.zeros((b, POOL_HIST, pool_in), F32)
        tq = _tile(t, (512, 256, 128, 64))
        qt, cast = _proj(h, wts["q"], attn_qk, "q", gain=p["q_norm"], scale=hd ** -0.5 * math.log2(math.e),
                         transposed_tile=tq, side_jobs=side_q)
        keep(names_q, cast)
        (v32, vt), cast = _proj(h, wts["v"], attn_v, "v", transposed_tile=tq, side_jobs=side_v)
        keep(names_v, cast)
        attn = _attn_prompt(qt, k3, vt, p["lams"], p["subln"], n_heads, lam_i)
    else:
        cache_pool, cache_k, cache_v = cache
        pos0 = cache_k.shape[1]
        hist = cache_pool
        q, cast = _proj(h, wts["q"], attn_qk, "q", gain=p["q_norm"], scale=hd ** -0.5, side_jobs=side_q)
        keep(names_q, cast)
        (v32, v16), cast = _proj(h, wts["v"], attn_v, "v", side_jobs=side_v)
        keep(names_v, cast)
        attn = _attn_cached(q.reshape(b, t, attn_qk), cache_k.reshape(b, pos0, 2 * n_heads, hd), k3,
                            cache_v, v16.reshape(b, t, attn_v), p["lams"], p["subln"], lam_i)
    hist0 = jnp.pad(hist, ((0, 0), (HIST_ROWS - POOL_HIST, 0), (0, 0)))
    mix = _poolmix(u3, hist0, wts["pool"], p["pool_scale"], gates.reshape(b, t, 2 * d), attn, pos0)

    x1 = _mm_res(mix.reshape(b * t, d), wts["out"], x2)
    h2 = _rmsnorm(x1, p["norm_ffn"])
    names, side = jobs(("ffn_out", p["w_ffn_out"], 0, d))
    act, cast = _swiglu(h2, wts["ffn_gate"], wts["ffn_up"], side_jobs=side)
    keep(names, cast)
    y = _mm_res(act, wts["ffn_out"], x1)

    new_pool = jnp.concatenate([hist, u3], axis=1)[:, -POOL_HIST:] if t < POOL_HIST else u3[:, t - POOL_HIST:]
    new_k = k32.reshape(b, t, n_heads, 2, hd)
    new_v = v32.reshape(b, t, n_heads, dv)
    return y.reshape(b, t, d), new_pool, new_k, new_v


def kernel(x_prompt, x_sample, cache_pool, cache_k, cache_v, norm_mix, w_in, w_pool, pool_scale, q_norm, k_norm,
           lambda_q1, lambda_k1, lambda_q2, lambda_k2, subln, w_out, norm_ffn, w_ffn_in, w_ffn_out):
    depth = w_in.shape[0]
    n_heads, hd, dv = cache_k.shape[3], cache_k.shape[5], cache_v.shape[4]
    xp, xs = x_prompt, x_sample
    outs_p, outs_s = [], []
    for l in range(depth):
        vec = lambda a: a[l].reshape(1, -1).astype(F32)
        p = dict(
            n_heads=n_heads, hd=hd, dv=dv,
            norm_mix=norm_mix[l], norm_ffn=norm_ffn[l], q_norm=q_norm[l], k_norm=k_norm[l],
            pool_scale=pool_scale[l], subln=vec(subln),
            lams=(vec(lambda_q1), vec(lambda_k1), vec(lambda_q2), vec(lambda_k2)),
            w_in=w_in[l], w_pool=w_pool[l], w_out=w_out[l], w_ffn_in=w_ffn_in[l], w_ffn_out=w_ffn_out[l],
        )
        wts = {}
        xp, pool_p, k_p, v_p = _layer(xp, None, l, p, wts)
        xs, pool_s, k_s, v_s = _layer(xs, (cache_pool[l], cache_k[l], cache_v[l]), l, p, wts)
        outs_p.append((pool_p, k_p, v_p))
        outs_s.append((pool_s, k_s, v_s))
    stack = lambda outs, i: outs[0][i][None] if depth == 1 else jnp.stack([o[i] for o in outs])
    return (xp, xs, stack(outs_p, 0), stack(outs_p, 1), stack(outs_p, 2),
            stack(outs_s, 0), stack(outs_s, 1), stack(outs_s, 2))
```

```python
import functools
import math

import jax
import jax.numpy as jnp
from jax import lax
from jax.experimental import pallas as pl
from jax.experimental.pallas import tpu as pltpu

F32 = jnp.float32
BF16 = jnp.bfloat16

CHUNK = 64
POOL_WINDOWS = (2, 4, 8, 16)
POOL_HIST = max(POOL_WINDOWS) - 1
HIST_ROWS = 16
EPS = 1e-6
LANES = 128
BF16_SUBLANES = 16

V7X_VMEM_BYTES = 64 * 1024 * 1024
VMEM_LIMIT = V7X_VMEM_BYTES - 6 * 1024 * 1024


def _params(n_grid_axes):
    return pltpu.CompilerParams(dimension_semantics=("arbitrary",) * n_grid_axes,
                                vmem_limit_bytes=VMEM_LIMIT)


def _tile(n, candidates):
    for c in candidates:
        if n % c == 0:
            return c
    return n


def _weight_specs(k, tn, n_row_tiles):
    parts = 1
    if n_row_tiles == 1:
        parts = next((p for p in (4, 2) if k % (p * LANES) == 0), 1)
    return [pl.BlockSpec((k // parts, tn), lambda i, j, s=s: (s, j)) for s in range(parts)]


def _slab_dot(a_ref, w_refs):
    kp = a_ref.shape[1] // len(w_refs)
    z = jnp.dot(a_ref[:, :kp], w_refs[0][...], preferred_element_type=F32)
    for s in range(1, len(w_refs)):
        z += jnp.dot(a_ref[:, s * kp:(s + 1) * kp], w_refs[s][...], preferred_element_type=F32)
    return z


def _lambda_init(layer):
    return 0.8 - 0.6 * math.exp(-0.3 * layer)


def _with_side_casts(body, n_in, n_side):
    def side_cast_kernel(*refs):
        ins, srcs, rest = refs[:n_in], refs[n_in:n_in + n_side], refs[n_in + n_side:]
        outs, dsts = rest[:len(rest) - n_side], rest[len(rest) - n_side:]
        body(*ins, *outs)
        for src, dst in zip(srcs, dsts):
            dst[...] = src[...].astype(dst.dtype)
    return side_cast_kernel


def _side_casts(jobs, grid):
    n_steps = math.prod(grid)

    def step_of(*g):
        step = g[0]
        for extent, idx in zip(grid[1:], g[1:]):
            step = step * extent + idx
        return step

    srcs, in_specs, out_specs, out_shapes, riding = [], [], [], [], []
    for src, col0, ncols in jobs:
        r = src.shape[0]
        wb = math.gcd(col0, ncols) if col0 else ncols
        ncb = ncols // wb
        rows = r * ncb // n_steps
        fits = (n_steps % ncb == 0 and r % (n_steps // ncb) == 0
                and rows % BF16_SUBLANES == 0 and wb % LANES == 0)
        riding.append(fits)
        if not fits:
            continue
        cb0 = col0 // wb
        srcs.append(src)
        in_specs.append(pl.BlockSpec(
            (rows, wb), lambda *g, ncb=ncb, cb0=cb0: (step_of(*g) // ncb, cb0 + step_of(*g) % ncb)))
        out_specs.append(pl.BlockSpec((rows, wb), lambda *g, ncb=ncb: (step_of(*g) // ncb, step_of(*g) % ncb)))
        out_shapes.append(jax.ShapeDtypeStruct((r, ncols), BF16))

    def finish(results):
        results = list(results)
        return [results.pop(0) if fits else src[:, col0:col0 + ncols].astype(BF16)
                for fits, (src, col0, ncols) in zip(riding, jobs)]

    return srcs, in_specs, out_specs, out_shapes, finish


def _rmsnorm_kernel(x_ref, g_ref, o_ref):
    x = x_ref[...]
    ms = jnp.mean(x * x, axis=-1, keepdims=True)
    o_ref[...] = (x * lax.rsqrt(ms + EPS) * g_ref[...]).astype(o_ref.dtype)


def _rmsnorm(x, g):
    m, d = x.shape
    tm = _tile(m, (256, 128, 64, 32, 16))
    return pl.pallas_call(
        _rmsnorm_kernel,
        grid=(m // tm,),
        in_specs=[pl.BlockSpec((tm, d), lambda i: (i, 0)),
                  pl.BlockSpec((1, d), lambda i: (0, 0))],
        out_specs=pl.BlockSpec((tm, d), lambda i: (i, 0)),
        out_shape=jax.ShapeDtypeStruct((m, d), BF16),
        compiler_params=_params(1),
        name="rmsnorm",
    )(x, g.reshape(1, d).astype(F32))


def _proj_kernel(a_ref, *refs, n_w, mode, scale, transposed):
    w_refs, g_ref, out_refs = refs[:n_w], refs[n_w], refs[n_w + 1:]
    z = _slab_dot(a_ref, w_refs)
    if mode == "u":
        out_refs[0][...] = z
    elif mode == "v":
        out_refs[0][...] = z
        if transposed:
            rows = out_refs[1].shape[-1]
            for r in range(out_refs[1].shape[0]):
                out_refs[1][r] = z[r * rows:(r + 1) * rows, :].T.astype(BF16)
        else:
            out_refs[1][...] = z.astype(BF16)
    elif mode == "gate":
        out_refs[0][...] = jax.nn.sigmoid(z).astype(BF16)
    else:
        g = g_ref[...]
        hd = g.shape[-1]
        for c in range(z.shape[1] // hd):
            sl = slice(c * hd, (c + 1) * hd)
            zc = z[:, sl]
            y = zc * lax.rsqrt(jnp.mean(zc * zc, axis=-1, keepdims=True) + EPS) * g
            if mode == "k":
                out_refs[0][:, c, :] = y
                out_refs[1][:, sl] = y.astype(BF16)
            elif transposed:
                rows = out_refs[0].shape[-1]
                for r in range(out_refs[0].shape[0]):
                    out_refs[0][r, sl, :] = (y[r * rows:(r + 1) * rows, :] * scale).T.astype(BF16)
            else:
                out_refs[0][:, sl] = (y * scale).astype(BF16)


def _proj(h, w, n, mode, gain=None, scale=1.0, transposed_tile=0, side_jobs=()):
    m, k = h.shape
    two_out = mode in ("k", "v")
    transposed = transposed_tile > 0
    tm = _tile(m, (512,) if mode == "k" else (1024, 512, 256, 128, 64, 32, 16))
    narrow = mode == "v" or (m == tm and mode != "k")
    tn = _tile(n, (512,) if narrow else (1024, 512, 256, 128))
    slabs = tm // transposed_tile if transposed else 1
    w_specs = _weight_specs(k, tn, m // tm)
    if gain is None:
        gain = jnp.ones((1, 128), F32)
    gain = gain.reshape(1, -1).astype(F32)
    o_spec = pl.BlockSpec((tm, tn), lambda i, j: (i, j))
    if mode == "u":
        out_shape, out_specs = [jax.ShapeDtypeStruct((m, n), F32)], [o_spec]
    elif mode == "v" and transposed:
        out_shape = [jax.ShapeDtypeStruct((m, n), F32),
                     jax.ShapeDtypeStruct((m // transposed_tile, n, transposed_tile), BF16)]
        out_specs = [o_spec, pl.BlockSpec((slabs, tn, transposed_tile), lambda i, j: (i, j, 0))]
    elif mode == "k":
        hd = gain.shape[-1]
        out_shape = [jax.ShapeDtypeStruct((m, n // hd, hd), F32), jax.ShapeDtypeStruct((m, n), BF16)]
        out_specs = [pl.BlockSpec((tm, tn // hd, hd), lambda i, j: (i, j, 0)), o_spec]
    elif two_out:
        out_shape = [jax.ShapeDtypeStruct((m, n), F32), jax.ShapeDtypeStruct((m, n), BF16)]
        out_specs = [o_spec, o_spec]
    elif transposed:
        out_shape = [jax.ShapeDtypeStruct((m // transposed_tile, n, transposed_tile), BF16)]
        out_specs = [pl.BlockSpec((slabs, tn, transposed_tile), lambda i, j: (i, j, 0))]
    else:
        out_shape, out_specs = [jax.ShapeDtypeStruct((m, n), BF16)], [o_spec]
    grid = (m // tm, n // tn)
    srcs, side_in, side_out, side_shapes, finish = _side_casts(side_jobs, grid)
    n_main = len(out_shape)
    outs = pl.pallas_call(
        _with_side_casts(functools.partial(_proj_kernel, n_w=len(w_specs), mode=mode, scale=scale,
                                           transposed=transposed),
                         2 + len(w_specs), len(srcs)),
        grid=grid,
        in_specs=[pl.BlockSpec((tm, k), lambda i, j: (i, 0)), *w_specs,
                  pl.BlockSpec(gain.shape, lambda i, j: (0, 0))] + side_in,
        out_specs=out_specs + side_out,
        out_shape=out_shape + side_shapes,
        compiler_params=_params(2),
        name="proj_" + mode,
    )(h, *[w] * len(w_specs), gain, *srcs)
    main = tuple(outs[:n_main]) if two_out else outs[0]
    return main, finish(outs[n_main:])


def _lambda(lq1, lk1, lq2, lk2, lam_i):
    a = jnp.sum(lq1[...] * lk1[...], axis=-1, keepdims=True)
    b = jnp.sum(lq2[...] * lk2[...], axis=-1, keepdims=True)
    return jnp.exp(a) - jnp.exp(b) + lam_i


def _head_out(o0, o1, lam, sub, lam_i):
    o = o0 - lam * o1
    y = o * lax.rsqrt(jnp.mean(o * o, axis=-1, keepdims=True) + EPS) * sub
    return y * (1.0 - lam_i)


def _nt_dot(a, b):
    return lax.dot_general(a, b, (((1,), (1,)), ((), ())), preferred_element_type=F32)


def _attn_prompt_kernel(qt_ref, k_ref, vt_ref, lq1, lk1, lq2, lk2, sub_ref, o_ref, acc_sc, sa_sc, sb_sc, sc_sc,
                        *, tq, hd, lam_i):
    nq = qt_ref.shape[0]
    lam = _lambda(lq1, lk1, lq2, lk2, lam_i)

    def put_scores(buf, qi, j):
        q = qt_ref[qi]
        k = k_ref[0, pl.ds(pl.multiple_of(j * tq, tq), tq), :]
        buf[:, :tq] = jnp.dot(k[:, :hd], q[:hd, :], preferred_element_type=F32)
        buf[:, tq:] = jnp.dot(k[:, hd:], q[hd:, :], preferred_element_type=F32)

    def update(j, s, m_old, l_old):
        m_new = jnp.maximum(m_old, jnp.max(s, axis=0, keepdims=True))
        alpha = jnp.exp2(m_old - m_new)
        p = jnp.exp2(s - m_new)
        l_new = alpha * l_old + jnp.sum(p, axis=0, keepdims=True)
        acc_sc[...] = alpha * acc_sc[...] + jnp.dot(vt_ref[j], p.astype(BF16), preferred_element_type=F32)
        return m_new, l_new

    def step(qi, j, cur, nxt, carry):
        put_scores(nxt, qi, j + 1)
        return update(j, cur[...], *carry)

    put_scores(sa_sc, 0, 0)

    def query_tile(qi, _):
        acc_sc[...] = jnp.zeros(acc_sc.shape, F32)
        carry = (jnp.full((1, 2 * tq), -jnp.inf, F32), jnp.zeros((1, 2 * tq), F32))
        peel_two = jnp.logical_and(qi >= 1, qi % 2 == 0).astype(jnp.int32)
        peel_one = (qi % 2 == 1).astype(jnp.int32)
        carry = lax.fori_loop(0, peel_one, lambda _, c: step(qi, 0, sc_sc, sa_sc, c), carry)
        carry = lax.fori_loop(
            0, peel_two, lambda _, c: step(qi, 1, sb_sc, sa_sc, step(qi, 0, sc_sc, sb_sc, c)), carry)
        j0 = 1 + peel_two

        def pair(j, c):
            return step(qi, j + 1, sb_sc, sa_sc, step(qi, j, sa_sc, sb_sc, c))

        n_pairs = jnp.maximum(qi - j0, 0) // 2
        carry = lax.fori_loop(0, n_pairs % 2, lambda _, c: pair(j0, c), carry)
        j1 = j0 + 2 * (n_pairs % 2)
        m, l = lax.fori_loop(0, n_pairs // 2, lambda jj, c: pair(j1 + 4 * jj + 2, pair(j1 + 4 * jj, c)), carry)

        put_scores(sc_sc, jnp.minimum(qi + 1, nq - 1), 0)
        q_chunk = lax.broadcasted_iota(jnp.int32, (tq, tq), 1) // CHUNK
        k_chunk = lax.broadcasted_iota(jnp.int32, (tq, tq), 0) // CHUNK
        mask = k_chunk <= q_chunk
        s = jnp.where(jnp.concatenate([mask, mask], axis=1), sa_sc[...], jnp.finfo(F32).min)
        m, l = update(qi, s, m, l)
        inv_l = 1.0 / l
        acc = acc_sc[...]
        o0 = (acc[:, :tq] * inv_l[:, :tq]).T
        o1 = (acc[:, tq:] * inv_l[:, tq:]).T
        rows = pl.ds(pl.multiple_of(qi * tq, tq), tq)
        o_ref[0, rows, :] = _head_out(o0, o1, lam, sub_ref[...], lam_i).astype(o_ref.dtype)
        return 0

    lax.fori_loop(0, nq, query_tile, 0)


def _attn_prompt(qt, k, vt, lams, subln, n_heads, lam_i):
    b, t, qcols = k.shape
    tq = qt.shape[-1]
    hd = qcols // (2 * n_heads)
    dv = vt.shape[1] // n_heads
    nq = t // tq
    vec = pl.BlockSpec((1, hd), lambda bi, h: (0, 0))
    return pl.pallas_call(
        functools.partial(_attn_prompt_kernel, tq=tq, hd=hd, lam_i=lam_i),
        grid=(b, n_heads),
        in_specs=[pl.BlockSpec((nq, 2 * hd, tq), lambda bi, h: (bi, h, 0)),
                  pl.BlockSpec((1, t, 2 * hd), lambda bi, h: (bi, 0, h)),
                  pl.BlockSpec((nq, dv, tq), lambda bi, h: (bi, h, 0)),
                  vec, vec, vec, vec,
                  pl.BlockSpec((1, dv), lambda bi, h: (0, 0))],
        out_specs=pl.BlockSpec((1, t, dv), lambda bi, h: (bi, 0, h)),
        out_shape=jax.ShapeDtypeStruct((b, t, n_heads * dv), BF16),
        scratch_shapes=[pltpu.VMEM((dv, 2 * tq), F32)] + [pltpu.VMEM((tq, 2 * tq), F32)] * 3,
        compiler_params=_params(2),
        name="attn_prompt",
    )(qt, k, vt, *lams, subln)


def _attn_cached_kernel(q_ref, kc_ref, kn_ref, vc_ref, vn_ref, lq1, lk1, lq2, lk2, sub_ref, o_ref,
                        *, hd, dv, lam_i):
    ts = q_ref.shape[1]
    past = kc_ref.shape[1]
    q_chunk = (past + lax.broadcasted_iota(jnp.int32, (ts, 1), 0)) // CHUNK
    mask_c = lax.broadcasted_iota(jnp.int32, (ts, past), 1) // CHUNK <= q_chunk
    mask_n = (past + lax.broadcasted_iota(jnp.int32, (ts, ts), 1)) // CHUNK <= q_chunk
    neg = jnp.finfo(F32).min
    lam = _lambda(lq1, lk1, lq2, lk2, lam_i)
    for h in range(vc_ref.shape[2]):
        v_cols = slice(h * dv, (h + 1) * dv)
        v_c = vc_ref[0, :, h, :].astype(BF16)
        v_n = vn_ref[0, :, v_cols]
        outs = []
        for half in range(2):
            g = 2 * h + half
            cols = slice(g * hd, (g + 1) * hd)
            q = q_ref[0, :, cols]
            sc = jnp.where(mask_c, _nt_dot(q, kc_ref[0, :, g, :].astype(BF16)), neg)
            sn = jnp.where(mask_n, _nt_dot(q, kn_ref[0, :, cols]), neg)
            m = jnp.maximum(jnp.max(sc, axis=-1, keepdims=True), jnp.max(sn, axis=-1, keepdims=True))
            pc = jnp.exp(sc - m)
            pn = jnp.exp(sn - m)
            l = jnp.sum(pc, axis=-1, keepdims=True) + jnp.sum(pn, axis=-1, keepdims=True)
            o = (jnp.dot(pc.astype(BF16), v_c, preferred_element_type=F32)
                 + jnp.dot(pn.astype(BF16), v_n, preferred_element_type=F32))
            outs.append(o * (1.0 / l))
        o_ref[0, :, v_cols] = _head_out(outs[0], outs[1], lam, sub_ref[...], lam_i).astype(o_ref.dtype)


def _attn_cached(q, k_cache, k_new, v_cache, v_new, lams, subln, lam_i):
    b, ts, _ = q.shape
    _, past, n_heads, dv = v_cache.shape
    hd = k_cache.shape[-1]
    hb = _tile(n_heads, (8,))
    vec = pl.BlockSpec((1, hd), lambda bi, h: (0, 0))
    return pl.pallas_call(
        functools.partial(_attn_cached_kernel, hd=hd, dv=dv, lam_i=lam_i),
        grid=(b, n_heads // hb),
        in_specs=[pl.BlockSpec((1, ts, hb * 2 * hd), lambda bi, h: (bi, 0, h)),
                  pl.BlockSpec((1, past, 2 * hb, hd), lambda bi, h: (bi, 0, h, 0)),
                  pl.BlockSpec((1, ts, hb * 2 * hd), lambda bi, h: (bi, 0, h)),
                  pl.BlockSpec((1, past, hb, dv), lambda bi, h: (bi, 0, h, 0)),
                  pl.BlockSpec((1, ts, hb * dv), lambda bi, h: (bi, 0, h)),
                  vec, vec, vec, vec,
                  pl.BlockSpec((1, dv), lambda bi, h: (0, 0))],
        out_specs=pl.BlockSpec((1, ts, hb * dv), lambda bi, h: (bi, 0, h)),
        out_shape=jax.ShapeDtypeStruct((b, ts, n_heads * dv), BF16),
        compiler_params=_params(2),
        name="attn_cached",
    )(q, k_cache, k_new, v_cache, v_new, *lams, subln)


def _poolmix_kernel(u_ref, uprev_ref, hist0_ref, wp_ref, ps_ref, ga_ref, gb_ref, at_ref, o_ref, ext_sc,
                    *, tm, pos0):
    i = pl.program_id(1)
    hist = jnp.where(i == 0, hist0_ref[0], uprev_ref[0])
    ext_sc[0:HIST_ROWS, :] = hist
    ext_sc[HIST_ROWS:HIST_ROWS + tm, :] = u_ref[0]
    pos = pos0 + i * tm + lax.broadcasted_iota(jnp.int32, (tm, 1), 0)
    gin = wp_ref.shape[1]
    gout = wp_ref.shape[2]
    for g, w in enumerate(POOL_WINDOWS):
        cin = slice(g * gin, (g + 1) * gin)
        cout = slice(g * gout, (g + 1) * gout)
        u_new = ext_sc[HIST_ROWS:HIST_ROWS + tm, cin]
        win = u_new
        for j in range(1, w):
            win = win + ext_sc[HIST_ROWS - j:HIST_ROWS - j + tm, cin]
        inv_cnt = 1.0 / jnp.minimum(pos + 1, w).astype(F32)
        pooled = win * inv_cnt - u_new
        y = jnp.dot(pooled.astype(BF16), wp_ref[g], preferred_element_type=F32) * ps_ref[:, cout]
        mix = ga_ref[0, :, cout].astype(F32) * y + gb_ref[0, :, cout].astype(F32) * at_ref[0, :, cout].astype(F32)
        o_ref[0, :, cout] = mix.astype(o_ref.dtype)


def _poolmix(u, hist0, w_pool, pool_scale, gates, attn, pos0):
    b, t, pin = u.shape
    d = attn.shape[-1]
    tm = _tile(t, (256, 128, 64, 32, 16))
    hb = tm // HIST_ROWS
    return pl.pallas_call(
        functools.partial(_poolmix_kernel, tm=tm, pos0=pos0),
        grid=(b, t // tm),
        in_specs=[pl.BlockSpec((1, tm, pin), lambda bi, i: (bi, i, 0)),
                  pl.BlockSpec((1, HIST_ROWS, pin), lambda bi, i: (bi, jnp.maximum(i * hb - 1, 0), 0)),
                  pl.BlockSpec((1, HIST_ROWS, pin), lambda bi, i: (bi, 0, 0)),
                  pl.BlockSpec(w_pool.shape, lambda bi, i: (0, 0, 0)),
                  pl.BlockSpec((1, d), lambda bi, i: (0, 0)),
                  pl.BlockSpec((1, tm, d), lambda bi, i: (bi, i, 0)),
                  pl.BlockSpec((1, tm, d), lambda bi, i: (bi, i, 1)),
                  pl.BlockSpec((1, tm, d), lambda bi, i: (bi, i, 0))],
        out_specs=pl.BlockSpec((1, tm, d), lambda bi, i: (bi, i, 0)),
        out_shape=jax.ShapeDtypeStruct((b, t, d), BF16),
        scratch_shapes=[pltpu.VMEM((HIST_ROWS + tm, pin), F32)],
        compiler_params=_params(2),
        name="poolmix",
    )(u, u, hist0, w_pool, pool_scale.reshape(1, d).astype(F32), gates, gates, attn)


def _mm_res_kernel(a_ref, *refs):
    *w_refs, r_ref, o_ref = refs
    o_ref[...] = r_ref[...] + _slab_dot(a_ref, w_refs)


def _mm_res(a, w, res):
    m, k = a.shape
    n = w.shape[1]
    big_k = k > 8192
    tm = _tile(m, (512,) if big_k else (1024, 512, 256, 128, 64, 32, 16))
    tn = _tile(n, (512, 256) if big_k or m == tm else (1024, 512, 256, 128))
    w_specs = _weight_specs(k, tn, m // tm)
    return pl.pallas_call(
        _mm_res_kernel,
        grid=(m // tm, n // tn),
        in_specs=[pl.BlockSpec((tm, k), lambda i, j: (i, 0)), *w_specs,
                  pl.BlockSpec((tm, tn), lambda i, j: (i, j))],
        out_specs=pl.BlockSpec((tm, tn), lambda i, j: (i, j)),
        out_shape=jax.ShapeDtypeStruct((m, n), F32),
        compiler_params=_params(2),
        name="mm_res",
    )(a, *[w] * len(w_specs), res)


def _swiglu_kernel(a_ref, *refs):
    n_w = (len(refs) - 1) // 2
    gate = _slab_dot(a_ref, refs[:n_w])
    up = _slab_dot(a_ref, refs[n_w:2 * n_w])
    o_ref = refs[-1]
    o_ref[...] = (gate * jax.nn.sigmoid(gate) * up).astype(o_ref.dtype)


def _swiglu(h, wg, wu, side_jobs=()):
    m, k = h.shape
    d_ff = wg.shape[1]
    tm = _tile(m, (2048, 1024, 512, 256, 128, 64, 32, 16))
    tn = _tile(d_ff, (256, 128))
    grid = (m // tm, d_ff // tn)
    srcs, side_in, side_out, side_shapes, finish = _side_casts(side_jobs, grid)
    w_specs = _weight_specs(k, tn, m // tm)
    outs = pl.pallas_call(
        _with_side_casts(_swiglu_kernel, 1 + 2 * len(w_specs), len(srcs)),
        grid=grid,
        in_specs=[pl.BlockSpec((tm, k), lambda i, j: (i, 0)), *w_specs, *w_specs] + side_in,
        out_specs=[pl.BlockSpec((tm, tn), lambda i, j: (i, j))] + side_out,
        out_shape=[jax.ShapeDtypeStruct((m, d_ff), BF16)] + side_shapes,
        compiler_params=_params(2),
        name="swiglu",
    )(h, *[wg] * len(w_specs), *[wu] * len(w_specs), *srcs)
    return outs[0], finish(outs[1:])


def _layer(x, cache, layer, p, wts):
    b, t, d = x.shape
    n_heads, hd, dv = p["n_heads"], p["hd"], p["dv"]
    pool_in = p["w_pool"].shape[0] * p["w_pool"].shape[1]
    attn_qk = n_heads * 2 * hd
    attn_v = n_heads * dv
    d_ff = p["w_ffn_out"].shape[0]
    o1 = pool_in
    o2 = o1 + attn_qk
    o3 = o2 + attn_qk
    o4 = o3 + attn_v
    lam_i = _lambda_init(layer)
    x2 = x.reshape(b * t, d)
    w_in, w_ffn_in = p["w_in"], p["w_ffn_in"]

    def jobs(*specs):
        todo = [sp for sp in specs if sp[0] not in wts]
        return [sp[0] for sp in todo], [sp[1:] for sp in todo]

    def keep(names, cast):
        wts.update(zip(names, cast))

    if "u" not in wts:
        wts["u"] = w_in[:, :o1].astype(BF16)
        wts["pool"] = p["w_pool"].astype(BF16)
    h = _rmsnorm(x2, p["norm_mix"])
    names, side = jobs(("k", w_in, o2, attn_qk))
    u, cast = _proj(h, wts["u"], pool_in, "u", side_jobs=side)
    keep(names, cast)
    names, side = jobs(("gate", w_in, o4, 2 * d))
    (k32, k16), cast = _proj(h, wts["k"], attn_qk, "k", gain=p["k_norm"], side_jobs=side)
    keep(names, cast)
    names, side = jobs(("q", w_in, o1, attn_qk), ("v", w_in, o3, attn_v), ("out", p["w_out"], 0, d))
    gates, cast = _proj(h, wts["gate"], 2 * d, "gate", side_jobs=side)
    keep(names, cast)
    u3 = u.reshape(b, t, pool_in)
    k3 = k16.reshape(b, t, attn_qk)
    names_q, side_q = jobs(("ffn_gate", w_ffn_in, 0, d_ff))
    names_v, side_v = jobs(("ffn_up", w_ffn_in, d_ff, d_ff))
    if cache is None:
        pos0 = 0
        hist = jnp.zeros((b, POOL_HIST, pool_in), F32)
        tq = _tile(t, (512, 256, 128, 64))
        qt, cast = _proj(h, wts["q"], attn_qk, "q", gain=p["q_norm"], scale=hd ** -0.5 * math.log2(math.e),
                         transposed_tile=tq, side_jobs=side_q)
        keep(names_q, cast)
        (v32, vt), cast = _proj(h, wts["v"], attn_v, "v", transposed_tile=tq, side_jobs=side_v)
        keep(names_v, cast)
        attn = _attn_prompt(qt, k3, vt, p["lams"], p["subln"], n_heads, lam_i)
    else:
        cache_pool, cache_k, cache_v = cache
        pos0 = cache_k.shape[1]
        hist = cache_pool
        q, cast = _proj(h, wts["q"], attn_qk, "q", gain=p["q_norm"], scale=hd ** -0.5, side_jobs=side_q)
        keep(names_q, cast)
        (v32, v16), cast = _proj(h, wts["v"], attn_v, "v", side_jobs=side_v)
        keep(names_v, cast)
        attn = _attn_cached(q.reshape(b, t, attn_qk), cache_k.reshape(b, pos0, 2 * n_heads, hd), k3,
                            cache_v, v16.reshape(b, t, attn_v), p["lams"], p["subln"], lam_i)
    hist0 = jnp.pad(hist, ((0, 0), (HIST_ROWS - POOL_HIST, 0), (0, 0)))
    mix = _poolmix(u3, hist0, wts["pool"], p["pool_scale"], gates.reshape(b, t, 2 * d), attn, pos0)

    x1 = _mm_res(mix.reshape(b * t, d), wts["out"], x2)
    h2 = _rmsnorm(x1, p["norm_ffn"])
    names, side = jobs(("ffn_out", p["w_ffn_out"], 0, d))
    act, cast = _swiglu(h2, wts["ffn_gate"], wts["ffn_up"], side_jobs=side)
    keep(names, cast)
    y = _mm_res(act, wts["ffn_out"], x1)

    new_pool = jnp.concatenate([hist, u3], axis=1)[:, -POOL_HIST:] if t < POOL_HIST else u3[:, t - POOL_HIST:]
    new_k = k32.reshape(b, t, n_heads, 2, hd)
    new_v = v32.reshape(b, t, n_heads, dv)
    return y.reshape(b, t, d), new_pool, new_k, new_v


def kernel(x_prompt, x_sample, cache_pool, cache_k, cache_v, norm_mix, w_in, w_pool, pool_scale, q_norm, k_norm,
           lambda_q1, lambda_k1, lambda_q2, lambda_k2, subln, w_out, norm_ffn, w_ffn_in, w_ffn_out):
    depth = w_in.shape[0]
    n_heads, hd, dv = cache_k.shape[3], cache_k.shape[5], cache_v.shape[4]
    xp, xs = x_prompt, x_sample
    outs_p, outs_s = [], []
    for l in range(depth):
        vec = lambda a: a[l].reshape(1, -1).astype(F32)
        p = dict(
            n_heads=n_heads, hd=hd, dv=dv,
            norm_mix=norm_mix[l], norm_ffn=norm_ffn[l], q_norm=q_norm[l], k_norm=k_norm[l],
            pool_scale=pool_scale[l], subln=vec(subln),
            lams=(vec(lambda_q1), vec(lambda_k1), vec(lambda_q2), vec(lambda_k2)),
            w_in=w_in[l], w_pool=w_pool[l], w_out=w_out[l], w_ffn_in=w_ffn_in[l], w_ffn_out=w_ffn_out[l],
        )
        wts = {}
        xp, pool_p, k_p, v_p = _layer(xp, None, l, p, wts)
        xs, pool_s, k_s, v_s = _layer(xs, (cache_pool[l], cache_k[l], cache_v[l]), l, p, wts)
        outs_p.append((pool_p, k_p, v_p))
        outs_s.append((pool_s, k_s, v_s))
    stack = lambda outs, i: outs[0][i][None] if depth == 1 else jnp.stack([o[i] for o in outs])
    return (xp, xs, stack(outs_p, 0), stack(outs_p, 1), stack(outs_p, 2),
            stack(outs_s, 0), stack(outs_s, 1), stack(outs_s, 2))
```

```python
import functools
import math

import jax
import jax.numpy as jnp
from jax import lax
from jax.experimental import pallas as pl
from jax.experimental.pallas import tpu as pltpu

F32 = jnp.float32
BF16 = jnp.bfloat16

CHUNK = 64
POOL_WINDOWS = (2, 4, 8, 16)
POOL_HIST = max(POOL_WINDOWS) - 1
HIST_ROWS = 16
EPS = 1e-6
LANES = 128
SUBLANES = 8
BF16_SUBLANES = 16

V7X_VMEM_BYTES = 64 * 1024 * 1024
VMEM_LIMIT = V7X_VMEM_BYTES - 6 * 1024 * 1024


def _params(n_grid_axes):
    return pltpu.CompilerParams(dimension_semantics=("arbitrary",) * n_grid_axes,
                                vmem_limit_bytes=VMEM_LIMIT)


def _tile(n, candidates):
    for c in candidates:
        if n % c == 0:
            return c
    return n


def _lambda_init(layer):
    return 0.8 - 0.6 * math.exp(-0.3 * layer)


def _with_side_casts(body, n_in, n_side):
    def side_cast_kernel(*refs):
        ins, srcs, rest = refs[:n_in], refs[n_in:n_in + n_side], refs[n_in + n_side:]
        outs, dsts = rest[:len(rest) - n_side], rest[len(rest) - n_side:]
        body(*ins, *outs)
        for src, dst in zip(srcs, dsts):
            dst[...] = src[...].astype(dst.dtype)
    return side_cast_kernel


def _side_casts(jobs, grid):
    n_steps = math.prod(grid)

    def step_of(*g):
        step = g[0]
        for extent, idx in zip(grid[1:], g[1:]):
            step = step * extent + idx
        return step

    srcs, in_specs, out_specs, out_shapes, riding = [], [], [], [], []
    for src, col0, ncols in jobs:
        r = src.shape[0]
        wb = math.gcd(col0, ncols) if col0 else ncols
        ncb = ncols // wb
        rows = r * ncb // n_steps
        fits = (n_steps % ncb == 0 and r % (n_steps // ncb) == 0
                and rows % BF16_SUBLANES == 0 and wb % LANES == 0)
        riding.append(fits)
        if not fits:
            continue
        cb0 = col0 // wb
        srcs.append(src)
        in_specs.append(pl.BlockSpec(
            (rows, wb), lambda *g, ncb=ncb, cb0=cb0: (step_of(*g) // ncb, cb0 + step_of(*g) % ncb)))
        out_specs.append(pl.BlockSpec((rows, wb), lambda *g, ncb=ncb: (step_of(*g) // ncb, step_of(*g) % ncb)))
        out_shapes.append(jax.ShapeDtypeStruct((r, ncols), BF16))

    def finish(results):
        results = list(results)
        return [results.pop(0) if fits else src[:, col0:col0 + ncols].astype(BF16)
                for fits, (src, col0, ncols) in zip(riding, jobs)]

    return srcs, in_specs, out_specs, out_shapes, finish


def _rmsnorm_kernel(x_ref, g_ref, o_ref):
    x = x_ref[...]
    ms = jnp.mean(x * x, axis=-1, keepdims=True)
    o_ref[...] = (x * lax.rsqrt(ms + EPS) * g_ref[...]).astype(o_ref.dtype)


def _rmsnorm(x, g):
    m, d = x.shape
    tm = _tile(m, (512, 256, 128, 64, 32, 16))
    return pl.pallas_call(
        _rmsnorm_kernel,
        grid=(m // tm,),
        in_specs=[pl.BlockSpec((tm, d), lambda i: (i, 0)),
                  pl.BlockSpec((1, d), lambda i: (0, 0))],
        out_specs=pl.BlockSpec((tm, d), lambda i: (i, 0)),
        out_shape=jax.ShapeDtypeStruct((m, d), BF16),
        compiler_params=_params(1),
        name="rmsnorm",
    )(x, g.reshape(1, d).astype(F32))


def _proj_kernel(a_ref, w_ref, g_ref, *out_refs, mode, scale, transposed):
    z = jnp.dot(a_ref[...], w_ref[...], preferred_element_type=F32)
    if mode == "u":
        out_refs[0][...] = z
    elif mode == "v":
        out_refs[0][...] = z
        if transposed:
            rows = out_refs[1].shape[-1]
            for r in range(out_refs[1].shape[0]):
                out_refs[1][r] = z[r * rows:(r + 1) * rows, :].T.astype(BF16)
        else:
            out_refs[1][...] = z.astype(BF16)
    elif mode == "gate":
        out_refs[0][...] = jax.nn.sigmoid(z).astype(BF16)
    else:
        g = g_ref[...]
        hd = g.shape[-1]
        for c in range(z.shape[1] // hd):
            sl = slice(c * hd, (c + 1) * hd)
            zc = z[:, sl]
            y = zc * lax.rsqrt(jnp.mean(zc * zc, axis=-1, keepdims=True) + EPS) * g
            if mode == "k":
                out_refs[0][:, c, :] = y
                out_refs[1][:, sl] = y.astype(BF16)
            elif transposed:
                rows = out_refs[0].shape[-1]
                for r in range(out_refs[0].shape[0]):
                    out_refs[0][r, sl, :] = (y[r * rows:(r + 1) * rows, :] * scale).T.astype(BF16)
            else:
                out_refs[0][:, sl] = (y * scale).astype(BF16)


def _proj(h, w, n, mode, gain=None, scale=1.0, transposed_tile=0, side_jobs=()):
    m, k = h.shape
    two_out = mode in ("k", "v")
    transposed = transposed_tile > 0
    tm = _tile(m, (512,) if mode == "k" else (1024, 512, 256, 128, 64, 32, 16))
    tn = _tile(n, (512,) if mode == "v" else (1024, 512, 256, 128))
    slabs = tm // transposed_tile if transposed else 1
    if gain is None:
        gain = jnp.ones((1, 128), F32)
    gain = gain.reshape(1, -1).astype(F32)
    o_spec = pl.BlockSpec((tm, tn), lambda i, j: (i, j))
    if mode == "u":
        out_shape, out_specs = [jax.ShapeDtypeStruct((m, n), F32)], [o_spec]
    elif mode == "v" and transposed:
        out_shape = [jax.ShapeDtypeStruct((m, n), F32),
                     jax.ShapeDtypeStruct((m // transposed_tile, n, transposed_tile), BF16)]
        out_specs = [o_spec, pl.BlockSpec((slabs, tn, transposed_tile), lambda i, j: (i, j, 0))]
    elif mode == "k":
        hd = gain.shape[-1]
        out_shape = [jax.ShapeDtypeStruct((m, n // hd, hd), F32), jax.ShapeDtypeStruct((m, n), BF16)]
        out_specs = [pl.BlockSpec((tm, tn // hd, hd), lambda i, j: (i, j, 0)), o_spec]
    elif two_out:
        out_shape = [jax.ShapeDtypeStruct((m, n), F32), jax.ShapeDtypeStruct((m, n), BF16)]
        out_specs = [o_spec, o_spec]
    elif transposed:
        out_shape = [jax.ShapeDtypeStruct((m // transposed_tile, n, transposed_tile), BF16)]
        out_specs = [pl.BlockSpec((slabs, tn, transposed_tile), lambda i, j: (i, j, 0))]
    else:
        out_shape, out_specs = [jax.ShapeDtypeStruct((m, n), BF16)], [o_spec]
    grid = (m // tm, n // tn)
    srcs, side_in, side_out, side_shapes, finish = _side_casts(side_jobs, grid)
    n_main = len(out_shape)
    outs = pl.pallas_call(
        _with_side_casts(functools.partial(_proj_kernel, mode=mode, scale=scale, transposed=transposed),
                         3, len(srcs)),
        grid=grid,
        in_specs=[pl.BlockSpec((tm, k), lambda i, j: (i, 0)),
                  pl.BlockSpec((k, tn), lambda i, j: (0, j)),
                  pl.BlockSpec(gain.shape, lambda i, j: (0, 0))] + side_in,
        out_specs=out_specs + side_out,
        out_shape=out_shape + side_shapes,
        compiler_params=_params(2),
        name="proj_" + mode,
    )(h, w, gain, *srcs)
    main = tuple(outs[:n_main]) if two_out else outs[0]
    return main, finish(outs[n_main:])


def _lambda(lq1, lk1, lq2, lk2, lam_i):
    a = jnp.sum(lq1[...] * lk1[...], axis=-1, keepdims=True)
    b = jnp.sum(lq2[...] * lk2[...], axis=-1, keepdims=True)
    return jnp.exp(a) - jnp.exp(b) + lam_i


def _head_out(o0, o1, lam, sub, lam_i):
    o = o0 - lam * o1
    y = o * lax.rsqrt(jnp.mean(o * o, axis=-1, keepdims=True) + EPS) * sub
    return y * (1.0 - lam_i)


def _nt_dot(a, b):
    return lax.dot_general(a, b, (((1,), (1,)), ((), ())), preferred_element_type=F32)


def _attn_prompt_kernel(qt_ref, k_ref, vt_ref, lq1, lk1, lq2, lk2, sub_ref, o_ref, acc_sc, sa_sc, sb_sc, sc_sc,
                        *, tq, hd, lam_i):
    nq = qt_ref.shape[0]
    lam = _lambda(lq1, lk1, lq2, lk2, lam_i)

    def put_scores(buf, qi, j):
        q = qt_ref[qi]
        k = k_ref[0, pl.ds(pl.multiple_of(j * tq, tq), tq), :]
        buf[:, :tq] = jnp.dot(k[:, :hd], q[:hd, :], preferred_element_type=F32)
        buf[:, tq:] = jnp.dot(k[:, hd:], q[hd:, :], preferred_element_type=F32)

    def update(j, s, m_old, l_old):
        m_new = jnp.maximum(m_old, jnp.max(s, axis=0, keepdims=True))
        alpha = jnp.exp2(m_old - m_new)
        p = jnp.exp2(s - m_new)
        l_new = alpha * l_old + jnp.sum(p, axis=0, keepdims=True)
        acc_sc[...] = alpha * acc_sc[...] + jnp.dot(vt_ref[j], p.astype(BF16), preferred_element_type=F32)
        return m_new, l_new

    def step(qi, j, cur, nxt, carry):
        put_scores(nxt, qi, j + 1)
        return update(j, cur[...], *carry)

    put_scores(sa_sc, 0, 0)

    def query_tile(qi, _):
        acc_sc[...] = jnp.zeros(acc_sc.shape, F32)
        carry = (jnp.full((1, 2 * tq), -jnp.inf, F32), jnp.zeros((1, 2 * tq), F32))
        peel_two = jnp.logical_and(qi >= 1, qi % 2 == 0).astype(jnp.int32)
        peel_one = (qi % 2 == 1).astype(jnp.int32)
        carry = lax.fori_loop(0, peel_one, lambda _, c: step(qi, 0, sc_sc, sa_sc, c), carry)
        carry = lax.fori_loop(
            0, peel_two, lambda _, c: step(qi, 1, sb_sc, sa_sc, step(qi, 0, sc_sc, sb_sc, c)), carry)
        j0 = 1 + peel_two

        def pair(j, c):
            return step(qi, j + 1, sb_sc, sa_sc, step(qi, j, sa_sc, sb_sc, c))

        def quad(j, c):
            return pair(j + 2, pair(j, c))

        n_pairs = jnp.maximum(qi - j0, 0) // 2
        carry = lax.fori_loop(0, n_pairs % 2, lambda _, c: pair(j0, c), carry)
        j1 = j0 + 2 * (n_pairs % 2)
        carry = lax.fori_loop(0, (n_pairs // 2) % 2, lambda _, c: quad(j1, c), carry)
        j2 = j1 + 4 * ((n_pairs // 2) % 2)
        m, l = lax.fori_loop(0, n_pairs // 4, lambda jj, c: quad(j2 + 8 * jj + 4, quad(j2 + 8 * jj, c)), carry)

        put_scores(sc_sc, jnp.minimum(qi + 1, nq - 1), 0)
        q_chunk = lax.broadcasted_iota(jnp.int32, (tq, tq), 1) // CHUNK
        k_chunk = lax.broadcasted_iota(jnp.int32, (tq, tq), 0) // CHUNK
        mask = k_chunk <= q_chunk
        s = jnp.where(jnp.concatenate([mask, mask], axis=1), sa_sc[...], jnp.finfo(F32).min)
        m, l = update(qi, s, m, l)
        inv_l = 1.0 / l
        acc = acc_sc[...]
        o0 = (acc[:, :tq] * inv_l[:, :tq]).T
        o1 = (acc[:, tq:] * inv_l[:, tq:]).T
        rows = pl.ds(pl.multiple_of(qi * tq, tq), tq)
        o_ref[0, rows, :] = _head_out(o0, o1, lam, sub_ref[...], lam_i).astype(o_ref.dtype)
        return 0

    lax.fori_loop(0, nq, query_tile, 0)


def _attn_prompt(qt, k, vt, lams, subln, n_heads, lam_i):
    b, t, qcols = k.shape
    tq = qt.shape[-1]
    hd = qcols // (2 * n_heads)
    dv = vt.shape[1] // n_heads
    nq = t // tq
    vec = pl.BlockSpec((1, hd), lambda bi, h: (0, 0))
    return pl.pallas_call(
        functools.partial(_attn_prompt_kernel, tq=tq, hd=hd, lam_i=lam_i),
        grid=(b, n_heads),
        in_specs=[pl.BlockSpec((nq, 2 * hd, tq), lambda bi, h: (bi, h, 0)),
                  pl.BlockSpec((1, t, 2 * hd), lambda bi, h: (bi, 0, h)),
                  pl.BlockSpec((nq, dv, tq), lambda bi, h: (bi, h, 0)),
                  vec, vec, vec, vec,
                  pl.BlockSpec((1, dv), lambda bi, h: (0, 0))],
        out_specs=pl.BlockSpec((1, t, dv), lambda bi, h: (bi, 0, h)),
        out_shape=jax.ShapeDtypeStruct((b, t, n_heads * dv), BF16),
        scratch_shapes=[pltpu.VMEM((dv, 2 * tq), F32)] + [pltpu.VMEM((tq, 2 * tq), F32)] * 3,
        compiler_params=_params(2),
        name="attn_prompt",
    )(qt, k, vt, *lams, subln)


def _attn_cached_kernel(q_ref, kc_ref, kn_ref, vc_ref, vn_ref, lq1, lk1, lq2, lk2, sub_ref, o_ref,
                        *, hd, dv, lam_i):
    ts = q_ref.shape[1]
    past = kc_ref.shape[1]
    q_chunk = (past + lax.broadcasted_iota(jnp.int32, (ts, 1), 0)) // CHUNK
    mask_c = lax.broadcasted_iota(jnp.int32, (ts, past), 1) // CHUNK <= q_chunk
    mask_n = (past + lax.broadcasted_iota(jnp.int32, (ts, ts), 1)) // CHUNK <= q_chunk
    neg = jnp.finfo(F32).min
    lam = _lambda(lq1, lk1, lq2, lk2, lam_i)
    for h in range(vc_ref.shape[2]):
        v_cols = slice(h * dv, (h + 1) * dv)
        v_c = vc_ref[0, :, h, :].astype(BF16)
        v_n = vn_ref[0, :, v_cols]
        outs = []
        for half in range(2):
            g = 2 * h + half
            cols = slice(g * hd, (g + 1) * hd)
            q = q_ref[0, :, cols]
            sc = jnp.where(mask_c, _nt_dot(q, kc_ref[0, :, g, :].astype(BF16)), neg)
            sn = jnp.where(mask_n, _nt_dot(q, kn_ref[0, :, cols]), neg)
            m = jnp.maximum(jnp.max(sc, axis=-1, keepdims=True), jnp.max(sn, axis=-1, keepdims=True))
            pc = jnp.exp(sc - m)
            pn = jnp.exp(sn - m)
            l = jnp.sum(pc, axis=-1, keepdims=True) + jnp.sum(pn, axis=-1, keepdims=True)
            o = (jnp.dot(pc.astype(BF16), v_c, preferred_element_type=F32)
                 + jnp.dot(pn.astype(BF16), v_n, preferred_element_type=F32))
            outs.append(o * (1.0 / l))
        o_ref[0, :, v_cols] = _head_out(outs[0], outs[1], lam, sub_ref[...], lam_i).astype(o_ref.dtype)


def _attn_cached(q, k_cache, k_new, v_cache, v_new, lams, subln, lam_i):
    b, ts, _ = q.shape
    _, past, n_heads, dv = v_cache.shape
    hd = k_cache.shape[-1]
    hb = _tile(n_heads, (SUBLANES,))
    vec = pl.BlockSpec((1, hd), lambda bi, h: (0, 0))
    return pl.pallas_call(
        functools.partial(_attn_cached_kernel, hd=hd, dv=dv, lam_i=lam_i),
        grid=(b, n_heads // hb),
        in_specs=[pl.BlockSpec((1, ts, hb * 2 * hd), lambda bi, h: (bi, 0, h)),
                  pl.BlockSpec((1, past, 2 * hb, hd), lambda bi, h: (bi, 0, h, 0)),
                  pl.BlockSpec((1, ts, hb * 2 * hd), lambda bi, h: (bi, 0, h)),
                  pl.BlockSpec((1, past, hb, dv), lambda bi, h: (bi, 0, h, 0)),
                  pl.BlockSpec((1, ts, hb * dv), lambda bi, h: (bi, 0, h)),
                  vec, vec, vec, vec,
                  pl.BlockSpec((1, dv), lambda bi, h: (0, 0))],
        out_specs=pl.BlockSpec((1, ts, hb * dv), lambda bi, h: (bi, 0, h)),
        out_shape=jax.ShapeDtypeStruct((b, ts, n_heads * dv), BF16),
        compiler_params=_params(2),
        name="attn_cached",
    )(q, k_cache, k_new, v_cache, v_new, *lams, subln)


def _poolmix_kernel(u_ref, uprev_ref, hist0_ref, wp_ref, ps_ref, ga_ref, gb_ref, at_ref, o_ref, ext_sc,
                    *, tm, pos0):
    i = pl.program_id(1)
    hist = jnp.where(i == 0, hist0_ref[0], uprev_ref[0])
    ext_sc[0:HIST_ROWS, :] = hist
    ext_sc[HIST_ROWS:HIST_ROWS + tm, :] = u_ref[0]
    pos = pos0 + i * tm + lax.broadcasted_iota(jnp.int32, (tm, 1), 0)
    gin = wp_ref.shape[1]
    gout = wp_ref.shape[2]
    for g, w in enumerate(POOL_WINDOWS):
        cin = slice(g * gin, (g + 1) * gin)
        cout = slice(g * gout, (g + 1) * gout)
        u_new = ext_sc[HIST_ROWS:HIST_ROWS + tm, cin]
        win = u_new
        for j in range(1, w):
            win = win + ext_sc[HIST_ROWS - j:HIST_ROWS - j + tm, cin]
        inv_cnt = 1.0 / jnp.minimum(pos + 1, w).astype(F32)
        pooled = win * inv_cnt - u_new
        y = jnp.dot(pooled.astype(BF16), wp_ref[g], preferred_element_type=F32) * ps_ref[:, cout]
        mix = ga_ref[0, :, cout].astype(F32) * y + gb_ref[0, :, cout].astype(F32) * at_ref[0, :, cout].astype(F32)
        o_ref[0, :, cout] = mix.astype(o_ref.dtype)


def _poolmix(u, hist0, w_pool, pool_scale, gates, attn, pos0):
    b, t, pin = u.shape
    d = attn.shape[-1]
    tm = _tile(t, (256, 128, 64, 32, 16))
    hb = tm // HIST_ROWS
    return pl.pallas_call(
        functools.partial(_poolmix_kernel, tm=tm, pos0=pos0),
        grid=(b, t // tm),
        in_specs=[pl.BlockSpec((1, tm, pin), lambda bi, i: (bi, i, 0)),
                  pl.BlockSpec((1, HIST_ROWS, pin), lambda bi, i: (bi, jnp.maximum(i * hb - 1, 0), 0)),
                  pl.BlockSpec((1, HIST_ROWS, pin), lambda bi, i: (bi, 0, 0)),
                  pl.BlockSpec(w_pool.shape, lambda bi, i: (0, 0, 0)),
                  pl.BlockSpec((1, d), lambda bi, i: (0, 0)),
                  pl.BlockSpec((1, tm, d), lambda bi, i: (bi, i, 0)),
                  pl.BlockSpec((1, tm, d), lambda bi, i: (bi, i, 1)),
                  pl.BlockSpec((1, tm, d), lambda bi, i: (bi, i, 0))],
        out_specs=pl.BlockSpec((1, tm, d), lambda bi, i: (bi, i, 0)),
        out_shape=jax.ShapeDtypeStruct((b, t, d), BF16),
        scratch_shapes=[pltpu.VMEM((HIST_ROWS + tm, pin), F32)],
        compiler_params=_params(2),
        name="poolmix",
    )(u, u, hist0, w_pool, pool_scale.reshape(1, d).astype(F32), gates, gates, attn)


def _mm_res_kernel(a_ref, w_ref, r_ref, o_ref):
    o_ref[...] = r_ref[...] + jnp.dot(a_ref[...], w_ref[...], preferred_element_type=F32)


def _mm_res(a, w, res):
    m, k = a.shape
    n = w.shape[1]
    big_k = k > 8192
    tm = _tile(m, (512,) if big_k else (1024, 512, 256, 128, 64, 32, 16))
    tn = _tile(n, (512, 256) if big_k else (1024, 512, 256, 128))
    return pl.pallas_call(
        _mm_res_kernel,
        grid=(m // tm, n // tn),
        in_specs=[pl.BlockSpec((tm, k), lambda i, j: (i, 0)),
                  pl.BlockSpec((k, tn), lambda i, j: (0, j)),
                  pl.BlockSpec((tm, tn), lambda i, j: (i, j))],
        out_specs=pl.BlockSpec((tm, tn), lambda i, j: (i, j)),
        out_shape=jax.ShapeDtypeStruct((m, n), F32),
        compiler_params=_params(2),
        name="mm_res",
    )(a, w, res)


def _swiglu_kernel(a_ref, wg_ref, wu_ref, o_ref):
    a = a_ref[...]
    gate = jnp.dot(a, wg_ref[...], preferred_element_type=F32)
    up = jnp.dot(a, wu_ref[...], preferred_element_type=F32)
    o_ref[...] = (gate * jax.nn.sigmoid(gate) * up).astype(o_ref.dtype)


def _swiglu(h, wg, wu, side_jobs=()):
    m, k = h.shape
    d_ff = wg.shape[1]
    tm = _tile(m, (2048, 1024, 512, 256, 128, 64, 32, 16))
    tn = _tile(d_ff, (256, 128))
    grid = (m // tm, d_ff // tn)
    srcs, side_in, side_out, side_shapes, finish = _side_casts(side_jobs, grid)
    w_spec = pl.BlockSpec((k, tn), lambda i, j: (0, j))
    outs = pl.pallas_call(
        _with_side_casts(_swiglu_kernel, 3, len(srcs)),
        grid=grid,
        in_specs=[pl.BlockSpec((tm, k), lambda i, j: (i, 0)), w_spec, w_spec] + side_in,
        out_specs=[pl.BlockSpec((tm, tn), lambda i, j: (i, j))] + side_out,
        out_shape=[jax.ShapeDtypeStruct((m, d_ff), BF16)] + side_shapes,
        compiler_params=_params(2),
        name="swiglu",
    )(h, wg, wu, *srcs)
    return outs[0], finish(outs[1:])


def _layer(x, cache, layer, p, wts):
    b, t, d = x.shape
    n_heads, hd, dv = p["n_heads"], p["hd"], p["dv"]
    pool_in = p["w_pool"].shape[0] * p["w_pool"].shape[1]
    attn_qk = n_heads * 2 * hd
    attn_v = n_heads * dv
    d_ff = p["w_ffn_out"].shape[0]
    o1 = pool_in
    o2 = o1 + attn_qk
    o3 = o2 + attn_qk
    o4 = o3 + attn_v
    lam_i = _lambda_init(layer)
    x2 = x.reshape(b * t, d)
    w_in, w_ffn_in = p["w_in"], p["w_ffn_in"]

    def jobs(*specs):
        todo = [sp for sp in specs if sp[0] not in wts]
        return [sp[0] for sp in todo], [sp[1:] for sp in todo]

    def keep(names, cast):
        wts.update(zip(names, cast))

    if "u" not in wts:
        wts["u"] = w_in[:, :o1].astype(BF16)
        wts["pool"] = p["w_pool"].astype(BF16)
    h = _rmsnorm(x2, p["norm_mix"])
    names, side = jobs(("k", w_in, o2, attn_qk))
    u, cast = _proj(h, wts["u"], pool_in, "u", side_jobs=side)
    keep(names, cast)
    names, side = jobs(("gate", w_in, o4, 2 * d))
    (k32, k16), cast = _proj(h, wts["k"], attn_qk, "k", gain=p["k_norm"], side_jobs=side)
    keep(names, cast)
    names, side = jobs(("q", w_in, o1, attn_qk), ("v", w_in, o3, attn_v), ("out", p["w_out"], 0, d))
    gates, cast = _proj(h, wts["gate"], 2 * d, "gate", side_jobs=side)
    keep(names, cast)
    u3 = u.reshape(b, t, pool_in)
    k3 = k16.reshape(b, t, attn_qk)
    names_q, side_q = jobs(("ffn_gate", w_ffn_in, 0, d_ff))
    names_v, side_v = jobs(("ffn_up", w_ffn_in, d_ff, d_ff))
    if cache is None:
        pos0 = 0
        hist = jnp.zeros((b, POOL_HIST, pool_in), F32)
        tq = _tile(t, (512, 256, 128, 64))
        qt, cast = _proj(h, wts["q"], attn_qk, "q", gain=p["q_norm"], scale=hd ** -0.5 * math.log2(math.e),
                         transposed_tile=tq, side_jobs=side_q)
        keep(names_q, cast)
        (v32, vt), cast = _proj(h, wts["v"], attn_v, "v", transposed_tile=tq, side_jobs=side_v)
        keep(names_v, cast)
        attn = _attn_prompt(qt, k3, vt, p["lams"], p["subln"], n_heads, lam_i)
    else:
        cache_pool, cache_k, cache_v = cache
        pos0 = cache_k.shape[1]
        hist = cache_pool
        q, cast = _proj(h, wts["q"], attn_qk, "q", gain=p["q_norm"], scale=hd ** -0.5, side_jobs=side_q)
        keep(names_q, cast)
        (v32, v16), cast = _proj(h, wts["v"], attn_v, "v", side_jobs=side_v)
        keep(names_v, cast)
        attn = _attn_cached(q.reshape(b, t, attn_qk), cache_k.reshape(b, pos0, 2 * n_heads, hd), k3,
                            cache_v, v16.reshape(b, t, attn_v), p["lams"], p["subln"], lam_i)
    hist0 = jnp.pad(hist, ((0, 0), (HIST_ROWS - POOL_HIST, 0), (0, 0)))
    mix = _poolmix(u3, hist0, wts["pool"], p["pool_scale"], gates.reshape(b, t, 2 * d), attn, pos0)

    x1 = _mm_res(mix.reshape(b * t, d), wts["out"], x2)
    h2 = _rmsnorm(x1, p["norm_ffn"])
    names, side = jobs(("ffn_out", p["w_ffn_out"], 0, d))
    act, cast = _swiglu(h2, wts["ffn_gate"], wts["ffn_up"], side_jobs=side)
    keep(names, cast)
    y = _mm_res(act, wts["ffn_out"], x1)

    new_pool = jnp.concatenate([hist, u3], axis=1)[:, -POOL_HIST:] if t < POOL_HIST else u3[:, t - POOL_HIST:]
    new_k = k32.reshape(b, t, n_heads, 2, hd)
    new_v = v32.reshape(b, t, n_heads, dv)
    return y.reshape(b, t, d), new_pool, new_k, new_v


def kernel(x_prompt, x_sample, cache_pool, cache_k, cache_v, norm_mix, w_in, w_pool, pool_scale, q_norm, k_norm,
           lambda_q1, lambda_k1, lambda_q2, lambda_k2, subln, w_out, norm_ffn, w_ffn_in, w_ffn_out):
    depth = w_in.shape[0]
    n_heads, hd, dv = cache_k.shape[3], cache_k.shape[5], cache_v.shape[4]
    xp, xs = x_prompt, x_sample
    outs_p, outs_s = [], []
    for l in range(depth):
        vec = lambda a: a[l].reshape(1, -1).astype(F32)
        p = dict(
            n_heads=n_heads, hd=hd, dv=dv,
            norm_mix=norm_mix[l], norm_ffn=norm_ffn[l], q_norm=q_norm[l], k_norm=k_norm[l],
            pool_scale=pool_scale[l], subln=vec(subln),
            lams=(vec(lambda_q1), vec(lambda_k1), vec(lambda_q2), vec(lambda_k2)),
            w_in=w_in[l], w_pool=w_pool[l], w_out=w_out[l], w_ffn_in=w_ffn_in[l], w_ffn_out=w_ffn_out[l],
        )
        wts = {}
        xp, pool_p, k_p, v_p = _layer(xp, None, l, p, wts)
        xs, pool_s, k_s, v_s = _layer(xs, (cache_pool[l], cache_k[l], cache_v[l]), l, p, wts)
        outs_p.append((pool_p, k_p, v_p))
        outs_s.append((pool_s, k_s, v_s))
    stack = lambda outs, i: outs[0][i][None] if depth == 1 else jnp.stack([o[i] for o in outs])
    return (xp, xs, stack(outs_p, 0), stack(outs_p, 1), stack(outs_p, 2),
            stack(outs_s, 0), stack(outs_s, 1), stack(outs_s, 2))
```

```python
import functools
import math

import jax
import jax.numpy as jnp
from jax import lax
from jax.experimental import pallas as pl
from jax.experimental.pallas import tpu as pltpu

F32 = jnp.float32
BF16 = jnp.bfloat16

CHUNK = 64
POOL_WINDOWS = (2, 4, 8, 16)
POOL_HIST = max(POOL_WINDOWS) - 1
HIST_ROWS = 16
EPS = 1e-6
LANES = 128
SUBLANES = 8
BF16_SUBLANES = 16

V7X_VMEM_BYTES = 64 * 1024 * 1024
VMEM_LIMIT = V7X_VMEM_BYTES - 6 * 1024 * 1024


def _params(n_grid_axes):
    return pltpu.CompilerParams(dimension_semantics=("arbitrary",) * n_grid_axes,
                                vmem_limit_bytes=VMEM_LIMIT)


def _tile(n, candidates):
    for c in candidates:
        if n % c == 0:
            return c
    return n


def _lambda_init(layer):
    return 0.8 - 0.6 * math.exp(-0.3 * layer)


def _with_side_casts(body, n_in, n_side):
    def side_cast_kernel(*refs):
        ins, srcs, rest = refs[:n_in], refs[n_in:n_in + n_side], refs[n_in + n_side:]
        outs, dsts = rest[:len(rest) - n_side], rest[len(rest) - n_side:]
        body(*ins, *outs)
        for src, dst in zip(srcs, dsts):
            dst[...] = src[...].astype(dst.dtype)
    return side_cast_kernel


def _side_casts(jobs, grid):
    n_steps = math.prod(grid)

    def step_of(*g):
        step = g[0]
        for extent, idx in zip(grid[1:], g[1:]):
            step = step * extent + idx
        return step

    srcs, in_specs, out_specs, out_shapes, riding = [], [], [], [], []
    for src, col0, ncols in jobs:
        r = src.shape[0]
        wb = math.gcd(col0, ncols) if col0 else ncols
        ncb = ncols // wb
        rows = r * ncb // n_steps
        fits = (n_steps % ncb == 0 and r % (n_steps // ncb) == 0
                and rows % BF16_SUBLANES == 0 and wb % LANES == 0)
        riding.append(fits)
        if not fits:
            continue
        cb0 = col0 // wb
        srcs.append(src)
        in_specs.append(pl.BlockSpec(
            (rows, wb), lambda *g, ncb=ncb, cb0=cb0: (step_of(*g) // ncb, cb0 + step_of(*g) % ncb)))
        out_specs.append(pl.BlockSpec((rows, wb), lambda *g, ncb=ncb: (step_of(*g) // ncb, step_of(*g) % ncb)))
        out_shapes.append(jax.ShapeDtypeStruct((r, ncols), BF16))

    def finish(results):
        results = list(results)
        return [results.pop(0) if fits else src[:, col0:col0 + ncols].astype(BF16)
                for fits, (src, col0, ncols) in zip(riding, jobs)]

    return srcs, in_specs, out_specs, out_shapes, finish


def _rmsnorm_kernel(x_ref, g_ref, o_ref):
    x = x_ref[...]
    ms = jnp.mean(x * x, axis=-1, keepdims=True)
    o_ref[...] = (x * lax.rsqrt(ms + EPS) * g_ref[...]).astype(o_ref.dtype)


def _rmsnorm(x, g):
    m, d = x.shape
    tm = _tile(m, (512, 256, 128, 64, 32, 16))
    return pl.pallas_call(
        _rmsnorm_kernel,
        grid=(m // tm,),
        in_specs=[pl.BlockSpec((tm, d), lambda i: (i, 0)),
                  pl.BlockSpec((1, d), lambda i: (0, 0))],
        out_specs=pl.BlockSpec((tm, d), lambda i: (i, 0)),
        out_shape=jax.ShapeDtypeStruct((m, d), BF16),
        compiler_params=_params(1),
        name="rmsnorm",
    )(x, g.reshape(1, d).astype(F32))


def _proj_kernel(a_ref, w_ref, g_ref, *out_refs, mode, scale, transposed):
    z = jnp.dot(a_ref[...], w_ref[...], preferred_element_type=F32)
    if mode == "u":
        out_refs[0][...] = z
    elif mode == "v":
        out_refs[0][...] = z
        if transposed:
            rows = out_refs[1].shape[-1]
            for r in range(out_refs[1].shape[0]):
                out_refs[1][r] = z[r * rows:(r + 1) * rows, :].T.astype(BF16)
        else:
            out_refs[1][...] = z.astype(BF16)
    elif mode == "gate":
        out_refs[0][...] = jax.nn.sigmoid(z).astype(BF16)
    else:
        g = g_ref[...]
        hd = g.shape[-1]
        for c in range(z.shape[1] // hd):
            sl = slice(c * hd, (c + 1) * hd)
            zc = z[:, sl]
            y = zc * lax.rsqrt(jnp.mean(zc * zc, axis=-1, keepdims=True) + EPS) * g
            if mode == "k":
                out_refs[0][:, c, :] = y
                out_refs[1][:, sl] = y.astype(BF16)
            elif transposed:
                rows = out_refs[0].shape[-1]
                for r in range(out_refs[0].shape[0]):
                    out_refs[0][r, sl, :] = (y[r * rows:(r + 1) * rows, :] * scale).T.astype(BF16)
            else:
                out_refs[0][:, sl] = (y * scale).astype(BF16)


def _proj(h, w, n, mode, gain=None, scale=1.0, transposed_tile=0, side_jobs=()):
    m, k = h.shape
    two_out = mode in ("k", "v")
    transposed = transposed_tile > 0
    tm = _tile(m, (512,) if mode == "k" else (1024, 512, 256, 128, 64, 32, 16))
    tn = _tile(n, (512,) if mode == "v" else (1024, 512, 256, 128))
    slabs = tm // transposed_tile if transposed else 1
    if gain is None:
        gain = jnp.ones((1, 128), F32)
    gain = gain.reshape(1, -1).astype(F32)
    o_spec = pl.BlockSpec((tm, tn), lambda i, j: (i, j))
    if mode == "u":
        out_shape, out_specs = [jax.ShapeDtypeStruct((m, n), F32)], [o_spec]
    elif mode == "v" and transposed:
        out_shape = [jax.ShapeDtypeStruct((m, n), F32),
                     jax.ShapeDtypeStruct((m // transposed_tile, n, transposed_tile), BF16)]
        out_specs = [o_spec, pl.BlockSpec((slabs, tn, transposed_tile), lambda i, j: (i, j, 0))]
    elif mode == "k":
        hd = gain.shape[-1]
        out_shape = [jax.ShapeDtypeStruct((m, n // hd, hd), F32), jax.ShapeDtypeStruct((m, n), BF16)]
        out_specs = [pl.BlockSpec((tm, tn // hd, hd), lambda i, j: (i, j, 0)), o_spec]
    elif two_out:
        out_shape = [jax.ShapeDtypeStruct((m, n), F32), jax.ShapeDtypeStruct((m, n), BF16)]
        out_specs = [o_spec, o_spec]
    elif transposed:
        out_shape = [jax.ShapeDtypeStruct((m // transposed_tile, n, transposed_tile), BF16)]
        out_specs = [pl.BlockSpec((slabs, tn, transposed_tile), lambda i, j: (i, j, 0))]
    else:
        out_shape, out_specs = [jax.ShapeDtypeStruct((m, n), BF16)], [o_spec]
    grid = (m // tm, n // tn)
    srcs, side_in, side_out, side_shapes, finish = _side_casts(side_jobs, grid)
    n_main = len(out_shape)
    outs = pl.pallas_call(
        _with_side_casts(functools.partial(_proj_kernel, mode=mode, scale=scale, transposed=transposed),
                         3, len(srcs)),
        grid=grid,
        in_specs=[pl.BlockSpec((tm, k), lambda i, j: (i, 0)),
                  pl.BlockSpec((k, tn), lambda i, j: (0, j)),
                  pl.BlockSpec(gain.shape, lambda i, j: (0, 0))] + side_in,
        out_specs=out_specs + side_out,
        out_shape=out_shape + side_shapes,
        compiler_params=_params(2),
        name="proj_" + mode,
    )(h, w, gain, *srcs)
    main = tuple(outs[:n_main]) if two_out else outs[0]
    return main, finish(outs[n_main:])


def _lambda(lq1, lk1, lq2, lk2, lam_i):
    a = jnp.sum(lq1[...] * lk1[...], axis=-1, keepdims=True)
    b = jnp.sum(lq2[...] * lk2[...], axis=-1, keepdims=True)
    return jnp.exp(a) - jnp.exp(b) + lam_i


def _head_out(o0, o1, lam, sub, lam_i):
    o = o0 - lam * o1
    y = o * lax.rsqrt(jnp.mean(o * o, axis=-1, keepdims=True) + EPS) * sub
    return y * (1.0 - lam_i)


def _nt_dot(a, b):
    return lax.dot_general(a, b, (((1,), (1,)), ((), ())), preferred_element_type=F32)


def _attn_prompt_kernel(qt_ref, k_ref, vt_ref, lq1, lk1, lq2, lk2, sub_ref, o_ref, acc_sc, sa_sc, sb_sc, sc_sc,
                        *, tq, hd, lam_i):
    nq = qt_ref.shape[0]
    lam = _lambda(lq1, lk1, lq2, lk2, lam_i)

    def put_scores(buf, qi, j):
        q = qt_ref[qi]
        k = k_ref[0, pl.ds(pl.multiple_of(j * tq, tq), tq), :]
        buf[:, :tq] = jnp.dot(k[:, :hd], q[:hd, :], preferred_element_type=F32)
        buf[:, tq:] = jnp.dot(k[:, hd:], q[hd:, :], preferred_element_type=F32)

    def update(j, s, m_old, l_old):
        m_new = jnp.maximum(m_old, jnp.max(s, axis=0, keepdims=True))
        alpha = jnp.exp2(m_old - m_new)
        p = jnp.exp2(s - m_new)
        l_new = alpha * l_old + jnp.sum(p, axis=0, keepdims=True)
        acc_sc[...] = alpha * acc_sc[...] + jnp.dot(vt_ref[j], p.astype(BF16), preferred_element_type=F32)
        return m_new, l_new

    def step(qi, j, cur, nxt, carry):
        put_scores(nxt, qi, j + 1)
        return update(j, cur[...], *carry)

    put_scores(sa_sc, 0, 0)

    def query_tile(qi, _):
        acc_sc[...] = jnp.zeros(acc_sc.shape, F32)
        carry = (jnp.full((1, 2 * tq), -jnp.inf, F32), jnp.zeros((1, 2 * tq), F32))
        peel_two = jnp.logical_and(qi >= 1, qi % 2 == 0).astype(jnp.int32)
        peel_one = (qi % 2 == 1).astype(jnp.int32)
        carry = lax.fori_loop(0, peel_one, lambda _, c: step(qi, 0, sc_sc, sa_sc, c), carry)
        carry = lax.fori_loop(
            0, peel_two, lambda _, c: step(qi, 1, sb_sc, sa_sc, step(qi, 0, sc_sc, sb_sc, c)), carry)
        j0 = 1 + peel_two

        def pair(j, c):
            return step(qi, j + 1, sb_sc, sa_sc, step(qi, j, sa_sc, sb_sc, c))

        def quad(j, c):
            return pair(j + 2, pair(j, c))

        n_pairs = jnp.maximum(qi - j0, 0) // 2
        carry = lax.fori_loop(0, n_pairs % 2, lambda _, c: pair(j0, c), carry)
        j1 = j0 + 2 * (n_pairs % 2)
        carry = lax.fori_loop(0, (n_pairs // 2) % 2, lambda _, c: quad(j1, c), carry)
        j2 = j1 + 4 * ((n_pairs // 2) % 2)
        m, l = lax.fori_loop(0, n_pairs // 4, lambda jj, c: quad(j2 + 8 * jj + 4, quad(j2 + 8 * jj, c)), carry)

        put_scores(sc_sc, jnp.minimum(qi + 1, nq - 1), 0)
        q_chunk = lax.broadcasted_iota(jnp.int32, (tq, tq), 1) // CHUNK
        k_chunk = lax.broadcasted_iota(jnp.int32, (tq, tq), 0) // CHUNK
        mask = k_chunk <= q_chunk
        s = jnp.where(jnp.concatenate([mask, mask], axis=1), sa_sc[...], jnp.finfo(F32).min)
        m, l = update(qi, s, m, l)
        inv_l = 1.0 / l
        acc = acc_sc[...]
        o0 = (acc[:, :tq] * inv_l[:, :tq]).T
        o1 = (acc[:, tq:] * inv_l[:, tq:]).T
        rows = pl.ds(pl.multiple_of(qi * tq, tq), tq)
        o_ref[0, rows, :] = _head_out(o0, o1, lam, sub_ref[...], lam_i).astype(o_ref.dtype)
        return 0

    lax.fori_loop(0, nq, query_tile, 0, unroll=2 if nq % 2 == 0 else 1)


def _attn_prompt(qt, k, vt, lams, subln, n_heads, lam_i):
    b, t, qcols = k.shape
    tq = qt.shape[-1]
    hd = qcols // (2 * n_heads)
    dv = vt.shape[1] // n_heads
    nq = t // tq
    vec = pl.BlockSpec((1, hd), lambda bi, h: (0, 0))
    return pl.pallas_call(
        functools.partial(_attn_prompt_kernel, tq=tq, hd=hd, lam_i=lam_i),
        grid=(b, n_heads),
        in_specs=[pl.BlockSpec((nq, 2 * hd, tq), lambda bi, h: (bi, h, 0)),
                  pl.BlockSpec((1, t, 2 * hd), lambda bi, h: (bi, 0, h)),
                  pl.BlockSpec((nq, dv, tq), lambda bi, h: (bi, h, 0)),
                  vec, vec, vec, vec,
                  pl.BlockSpec((1, dv), lambda bi, h: (0, 0))],
        out_specs=pl.BlockSpec((1, t, dv), lambda bi, h: (bi, 0, h)),
        out_shape=jax.ShapeDtypeStruct((b, t, n_heads * dv), BF16),
        scratch_shapes=[pltpu.VMEM((dv, 2 * tq), F32)] + [pltpu.VMEM((tq, 2 * tq), F32)] * 3,
        compiler_params=_params(2),
        name="attn_prompt",
    )(qt, k, vt, *lams, subln)


def _attn_cached_kernel(q_ref, kc_ref, kn_ref, vc_ref, vn_ref, lq1, lk1, lq2, lk2, sub_ref, o_ref,
                        *, hd, dv, lam_i):
    ts = q_ref.shape[1]
    past = kc_ref.shape[1]
    q_chunk = (past + lax.broadcasted_iota(jnp.int32, (ts, 1), 0)) // CHUNK
    mask_c = lax.broadcasted_iota(jnp.int32, (ts, past), 1) // CHUNK <= q_chunk
    mask_n = (past + lax.broadcasted_iota(jnp.int32, (ts, ts), 1)) // CHUNK <= q_chunk
    neg = jnp.finfo(F32).min
    lam = _lambda(lq1, lk1, lq2, lk2, lam_i)
    for h in range(vc_ref.shape[2]):
        v_cols = slice(h * dv, (h + 1) * dv)
        v_c = vc_ref[0, :, h, :].astype(BF16)
        v_n = vn_ref[0, :, v_cols]
        outs = []
        for half in range(2):
            g = 2 * h + half
            cols = slice(g * hd, (g + 1) * hd)
            q = q_ref[0, :, cols]
            sc = jnp.where(mask_c, _nt_dot(q, kc_ref[0, :, g, :].astype(BF16)), neg)
            sn = jnp.where(mask_n, _nt_dot(q, kn_ref[0, :, cols]), neg)
            m = jnp.maximum(jnp.max(sc, axis=-1, keepdims=True), jnp.max(sn, axis=-1, keepdims=True))
            pc = jnp.exp(sc - m)
            pn = jnp.exp(sn - m)
            l = jnp.sum(pc, axis=-1, keepdims=True) + jnp.sum(pn, axis=-1, keepdims=True)
            o = (jnp.dot(pc.astype(BF16), v_c, preferred_element_type=F32)
                 + jnp.dot(pn.astype(BF16), v_n, preferred_element_type=F32))
            outs.append(o * (1.0 / l))
        o_ref[0, :, v_cols] = _head_out(outs[0], outs[1], lam, sub_ref[...], lam_i).astype(o_ref.dtype)


def _attn_cached(q, k_cache, k_new, v_cache, v_new, lams, subln, lam_i):
    b, ts, _ = q.shape
    _, past, n_heads, dv = v_cache.shape
    hd = k_cache.shape[-1]
    hb = _tile(n_heads, (SUBLANES,))
    vec = pl.BlockSpec((1, hd), lambda bi, h: (0, 0))
    return pl.pallas_call(
        functools.partial(_attn_cached_kernel, hd=hd, dv=dv, lam_i=lam_i),
        grid=(b, n_heads // hb),
        in_specs=[pl.BlockSpec((1, ts, hb * 2 * hd), lambda bi, h: (bi, 0, h)),
                  pl.BlockSpec((1, past, 2 * hb, hd), lambda bi, h: (bi, 0, h, 0)),
                  pl.BlockSpec((1, ts, hb * 2 * hd), lambda bi, h: (bi, 0, h)),
                  pl.BlockSpec((1, past, hb, dv), lambda bi, h: (bi, 0, h, 0)),
                  pl.BlockSpec((1, ts, hb * dv), lambda bi, h: (bi, 0, h)),
                  vec, vec, vec, vec,
                  pl.BlockSpec((1, dv), lambda bi, h: (0, 0))],
        out_specs=pl.BlockSpec((1, ts, hb * dv), lambda bi, h: (bi, 0, h)),
        out_shape=jax.ShapeDtypeStruct((b, ts, n_heads * dv), BF16),
        compiler_params=_params(2),
        name="attn_cached",
    )(q, k_cache, k_new, v_cache, v_new, *lams, subln)


def _poolmix_kernel(u_ref, uprev_ref, hist0_ref, wp_ref, ps_ref, ga_ref, gb_ref, at_ref, o_ref, ext_sc,
                    *, tm, pos0):
    i = pl.program_id(1)
    hist = jnp.where(i == 0, hist0_ref[0], uprev_ref[0])
    ext_sc[0:HIST_ROWS, :] = hist
    ext_sc[HIST_ROWS:HIST_ROWS + tm, :] = u_ref[0]
    pos = pos0 + i * tm + lax.broadcasted_iota(jnp.int32, (tm, 1), 0)
    gin = wp_ref.shape[1]
    gout = wp_ref.shape[2]
    for g, w in enumerate(POOL_WINDOWS):
        cin = slice(g * gin, (g + 1) * gin)
        cout = slice(g * gout, (g + 1) * gout)
        u_new = ext_sc[HIST_ROWS:HIST_ROWS + tm, cin]
        win = u_new
        for j in range(1, w):
            win = win + ext_sc[HIST_ROWS - j:HIST_ROWS - j + tm, cin]
        inv_cnt = 1.0 / jnp.minimum(pos + 1, w).astype(F32)
        pooled = win * inv_cnt - u_new
        y = jnp.dot(pooled.astype(BF16), wp_ref[g], preferred_element_type=F32) * ps_ref[:, cout]
        mix = ga_ref[0, :, cout].astype(F32) * y + gb_ref[0, :, cout].astype(F32) * at_ref[0, :, cout].astype(F32)
        o_ref[0, :, cout] = mix.astype(o_ref.dtype)


def _poolmix(u, hist0, w_pool, pool_scale, gates, attn, pos0):
    b, t, pin = u.shape
    d = attn.shape[-1]
    tm = _tile(t, (256, 128, 64, 32, 16))
    hb = tm // HIST_ROWS
    return pl.pallas_call(
        functools.partial(_poolmix_kernel, tm=tm, pos0=pos0),
        grid=(b, t // tm),
        in_specs=[pl.BlockSpec((1, tm, pin), lambda bi, i: (bi, i, 0)),
                  pl.BlockSpec((1, HIST_ROWS, pin), lambda bi, i: (bi, jnp.maximum(i * hb - 1, 0), 0)),
                  pl.BlockSpec((1, HIST_ROWS, pin), lambda bi, i: (bi, 0, 0)),
                  pl.BlockSpec(w_pool.shape, lambda bi, i: (0, 0, 0)),
                  pl.BlockSpec((1, d), lambda bi, i: (0, 0)),
                  pl.BlockSpec((1, tm, d), lambda bi, i: (bi, i, 0)),
                  pl.BlockSpec((1, tm, d), lambda bi, i: (bi, i, 1)),
                  pl.BlockSpec((1, tm, d), lambda bi, i: (bi, i, 0))],
        out_specs=pl.BlockSpec((1, tm, d), lambda bi, i: (bi, i, 0)),
        out_shape=jax.ShapeDtypeStruct((b, t, d), BF16),
        scratch_shapes=[pltpu.VMEM((HIST_ROWS + tm, pin), F32)],
        compiler_params=_params(2),
        name="poolmix",
    )(u, u, hist0, w_pool, pool_scale.reshape(1, d).astype(F32), gates, gates, attn)


def _mm_res_kernel(a_ref, w_ref, r_ref, o_ref):
    o_ref[...] = r_ref[...] + jnp.dot(a_ref[...], w_ref[...], preferred_element_type=F32)


def _mm_res(a, w, res):
    m, k = a.shape
    n = w.shape[1]
    big_k = k > 8192
    tm = _tile(m, (512,) if big_k else (1024, 512, 256, 128, 64, 32, 16))
    tn = _tile(n, (512, 256) if big_k else (1024, 512, 256, 128))
    return pl.pallas_call(
        _mm_res_kernel,
        grid=(m // tm, n // tn),
        in_specs=[pl.BlockSpec((tm, k), lambda i, j: (i, 0)),
                  pl.BlockSpec((k, tn), lambda i, j: (0, j)),
                  pl.BlockSpec((tm, tn), lambda i, j: (i, j))],
        out_specs=pl.BlockSpec((tm, tn), lambda i, j: (i, j)),
        out_shape=jax.ShapeDtypeStruct((m, n), F32),
        compiler_params=_params(2),
        name="mm_res",
    )(a, w, res)


def _swiglu_kernel(a_ref, wg_ref, wu_ref, o_ref):
    a = a_ref[...]
    gate = jnp.dot(a, wg_ref[...], preferred_element_type=F32)
    up = jnp.dot(a, wu_ref[...], preferred_element_type=F32)
    o_ref[...] = (gate * jax.nn.sigmoid(gate) * up).astype(o_ref.dtype)


def _swiglu(h, wg, wu, side_jobs=()):
    m, k = h.shape
    d_ff = wg.shape[1]
    tm = _tile(m, (2048, 1024, 512, 256, 128, 64, 32, 16))
    tn = _tile(d_ff, (256, 128))
    grid = (m // tm, d_ff // tn)
    srcs, side_in, side_out, side_shapes, finish = _side_casts(side_jobs, grid)
    w_spec = pl.BlockSpec((k, tn), lambda i, j: (0, j))
    outs = pl.pallas_call(
        _with_side_casts(_swiglu_kernel, 3, len(srcs)),
        grid=grid,
        in_specs=[pl.BlockSpec((tm, k), lambda i, j: (i, 0)), w_spec, w_spec] + side_in,
        out_specs=[pl.BlockSpec((tm, tn), lambda i, j: (i, j))] + side_out,
        out_shape=[jax.ShapeDtypeStruct((m, d_ff), BF16)] + side_shapes,
        compiler_params=_params(2),
        name="swiglu",
    )(h, wg, wu, *srcs)
    return outs[0], finish(outs[1:])


def _layer(x, cache, layer, p, wts):
    b, t, d = x.shape
    n_heads, hd, dv = p["n_heads"], p["hd"], p["dv"]
    pool_in = p["w_pool"].shape[0] * p["w_pool"].shape[1]
    attn_qk = n_heads * 2 * hd
    attn_v = n_heads * dv
    d_ff = p["w_ffn_out"].shape[0]
    o1 = pool_in
    o2 = o1 + attn_qk
    o3 = o2 + attn_qk
    o4 = o3 + attn_v
    lam_i = _lambda_init(layer)
    x2 = x.reshape(b * t, d)
    w_in, w_ffn_in = p["w_in"], p["w_ffn_in"]

    def jobs(*specs):
        todo = [sp for sp in specs if sp[0] not in wts]
        return [sp[0] for sp in todo], [sp[1:] for sp in todo]

    def keep(names, cast):
        wts.update(zip(names, cast))

    if "u" not in wts:
        wts["u"] = w_in[:, :o1].astype(BF16)
        wts["pool"] = p["w_pool"].astype(BF16)
    h = _rmsnorm(x2, p["norm_mix"])
    names, side = jobs(("k", w_in, o2, attn_qk))
    u, cast = _proj(h, wts["u"], pool_in, "u", side_jobs=side)
    keep(names, cast)
    names, side = jobs(("gate", w_in, o4, 2 * d))
    (k32, k16), cast = _proj(h, wts["k"], attn_qk, "k", gain=p["k_norm"], side_jobs=side)
    keep(names, cast)
    names, side = jobs(("q", w_in, o1, attn_qk), ("v", w_in, o3, attn_v), ("out", p["w_out"], 0, d))
    gates, cast = _proj(h, wts["gate"], 2 * d, "gate", side_jobs=side)
    keep(names, cast)
    u3 = u.reshape(b, t, pool_in)
    k3 = k16.reshape(b, t, attn_qk)
    names_q, side_q = jobs(("ffn_gate", w_ffn_in, 0, d_ff))
    names_v, side_v = jobs(("ffn_up", w_ffn_in, d_ff, d_ff))
    if cache is None:
        pos0 = 0
        hist = jnp.zeros((b, POOL_HIST, pool_in), F32)
        tq = _tile(t, (512, 256, 128, 64))
        qt, cast = _proj(h, wts["q"], attn_qk, "q", gain=p["q_norm"], scale=hd ** -0.5 * math.log2(math.e),
                         transposed_tile=tq, side_jobs=side_q)
        keep(names_q, cast)
        (v32, vt), cast = _proj(h, wts["v"], attn_v, "v", transposed_tile=tq, side_jobs=side_v)
        keep(names_v, cast)
        attn = _attn_prompt(qt, k3, vt, p["lams"], p["subln"], n_heads, lam_i)
    else:
        cache_pool, cache_k, cache_v = cache
        pos0 = cache_k.shape[1]
        hist = cache_pool
        q, cast = _proj(h, wts["q"], attn_qk, "q", gain=p["q_norm"], scale=hd ** -0.5, side_jobs=side_q)
        keep(names_q, cast)
        (v32, v16), cast = _proj(h, wts["v"], attn_v, "v", side_jobs=side_v)
        keep(names_v, cast)
        attn = _attn_cached(q.reshape(b, t, attn_qk), cache_k.reshape(b, pos0, 2 * n_heads, hd), k3,
                            cache_v, v16.reshape(b, t, attn_v), p["lams"], p["subln"], lam_i)
    hist0 = jnp.pad(hist, ((0, 0), (HIST_ROWS - POOL_HIST, 0), (0, 0)))
    mix = _poolmix(u3, hist0, wts["pool"], p["pool_scale"], gates.reshape(b, t, 2 * d), attn, pos0)

    x1 = _mm_res(mix.reshape(b * t, d), wts["out"], x2)
    h2 = _rmsnorm(x1, p["norm_ffn"])
    names, side = jobs(("ffn_out", p["w_ffn_out"], 0, d))
    act, cast = _swiglu(h2, wts["ffn_gate"], wts["ffn_up"], side_jobs=side)
    keep(names, cast)
    y = _mm_res(act, wts["ffn_out"], x1)

    new_pool = jnp.concatenate([hist, u3], axis=1)[:, -POOL_HIST:] if t < POOL_HIST else u3[:, t - POOL_HIST:]
    new_k = k32.reshape(b, t, n_heads, 2, hd)
    new_v = v32.reshape(b, t, n_heads, dv)
    return y.reshape(b, t, d), new_pool, new_k, new_v


def kernel(x_prompt, x_sample, cache_pool, cache_k, cache_v, norm_mix, w_in, w_pool, pool_scale, q_norm, k_norm,
           lambda_q1, lambda_k1, lambda_q2, lambda_k2, subln, w_out, norm_ffn, w_ffn_in, w_ffn_out):
    depth = w_in.shape[0]
    n_heads, hd, dv = cache_k.shape[3], cache_k.shape[5], cache_v.shape[4]
    xp, xs = x_prompt, x_sample
    outs_p, outs_s = [], []
    for l in range(depth):
        vec = lambda a: a[l].reshape(1, -1).astype(F32)
        p = dict(
            n_heads=n_heads, hd=hd, dv=dv,
            norm_mix=norm_mix[l], norm_ffn=norm_ffn[l], q_norm=q_norm[l], k_norm=k_norm[l],
            pool_scale=pool_scale[l], subln=vec(subln),
            lams=(vec(lambda_q1), vec(lambda_k1), vec(lambda_q2), vec(lambda_k2)),
            w_in=w_in[l], w_pool=w_pool[l], w_out=w_out[l], w_ffn_in=w_ffn_in[l], w_ffn_out=w_ffn_out[l],
        )
        wts = {}
        xp, pool_p, k_p, v_p = _layer(xp, None, l, p, wts)
        xs, pool_s, k_s, v_s = _layer(xs, (cache_pool[l], cache_k[l], cache_v[l]), l, p, wts)
        outs_p.append((pool_p, k_p, v_p))
        outs_s.append((pool_s, k_s, v_s))
    stack = lambda outs, i: outs[0][i][None] if depth == 1 else jnp.stack([o[i] for o in outs])
    return (xp, xs, stack(outs_p, 0), stack(outs_p, 1), stack(outs_p, 2),
            stack(outs_s, 0), stack(outs_s, 1), stack(outs_s, 2))
```

```python
import functools
import math

import jax
import jax.numpy as jnp
from jax import lax
from jax.experimental import pallas as pl
from jax.experimental.pallas import tpu as pltpu

F32 = jnp.float32
BF16 = jnp.bfloat16

CHUNK = 64
POOL_WINDOWS = (2, 4, 8, 16)
POOL_HIST = max(POOL_WINDOWS) - 1
HIST_ROWS = 16
EPS = 1e-6
LANES = 128
SUBLANES = 8
BF16_SUBLANES = 16

V7X_VMEM_BYTES = 64 * 1024 * 1024
VMEM_LIMIT = V7X_VMEM_BYTES - 6 * 1024 * 1024


def _params(n_grid_axes):
    return pltpu.CompilerParams(dimension_semantics=("arbitrary",) * n_grid_axes,
                                vmem_limit_bytes=VMEM_LIMIT)


def _tile(n, candidates):
    for c in candidates:
        if n % c == 0:
            return c
    return n


def _lambda_init(layer):
    return 0.8 - 0.6 * math.exp(-0.3 * layer)


def _with_side_casts(body, n_in, n_side):
    def side_cast_kernel(*refs):
        ins, srcs, rest = refs[:n_in], refs[n_in:n_in + n_side], refs[n_in + n_side:]
        outs, dsts = rest[:len(rest) - n_side], rest[len(rest) - n_side:]
        body(*ins, *outs)
        for src, dst in zip(srcs, dsts):
            dst[...] = src[...].astype(dst.dtype)
    return side_cast_kernel


def _side_casts(jobs, grid):
    n_steps = math.prod(grid)

    def step_of(*g):
        step = g[0]
        for extent, idx in zip(grid[1:], g[1:]):
            step = step * extent + idx
        return step

    srcs, in_specs, out_specs, out_shapes, riding = [], [], [], [], []
    for src, col0, ncols in jobs:
        r = src.shape[0]
        wb = math.gcd(col0, ncols) if col0 else ncols
        ncb = ncols // wb
        rows = r * ncb // n_steps
        fits = (n_steps % ncb == 0 and r % (n_steps // ncb) == 0
                and rows % BF16_SUBLANES == 0 and wb % LANES == 0)
        riding.append(fits)
        if not fits:
            continue
        cb0 = col0 // wb
        srcs.append(src)
        in_specs.append(pl.BlockSpec(
            (rows, wb), lambda *g, ncb=ncb, cb0=cb0: (step_of(*g) // ncb, cb0 + step_of(*g) % ncb)))
        out_specs.append(pl.BlockSpec((rows, wb), lambda *g, ncb=ncb: (step_of(*g) // ncb, step_of(*g) % ncb)))
        out_shapes.append(jax.ShapeDtypeStruct((r, ncols), BF16))

    def finish(results):
        results = list(results)
        return [results.pop(0) if fits else src[:, col0:col0 + ncols].astype(BF16)
                for fits, (src, col0, ncols) in zip(riding, jobs)]

    return srcs, in_specs, out_specs, out_shapes, finish


def _rmsnorm_kernel(x_ref, g_ref, o_ref):
    x = x_ref[...]
    ms = jnp.mean(x * x, axis=-1, keepdims=True)
    o_ref[...] = (x * lax.rsqrt(ms + EPS) * g_ref[...]).astype(o_ref.dtype)


def _rmsnorm(x, g):
    m, d = x.shape
    tm = _tile(m, (512, 256, 128, 64, 32, 16))
    return pl.pallas_call(
        _rmsnorm_kernel,
        grid=(m // tm,),
        in_specs=[pl.BlockSpec((tm, d), lambda i: (i, 0)),
                  pl.BlockSpec((1, d), lambda i: (0, 0))],
        out_specs=pl.BlockSpec((tm, d), lambda i: (i, 0)),
        out_shape=jax.ShapeDtypeStruct((m, d), BF16),
        compiler_params=_params(1),
        name="rmsnorm",
    )(x, g.reshape(1, d).astype(F32))


def _proj_kernel(a_ref, w_ref, g_ref, *out_refs, mode, scale, transposed):
    z = jnp.dot(a_ref[...], w_ref[...], preferred_element_type=F32)
    if mode == "u":
        out_refs[0][...] = z
    elif mode == "v":
        out_refs[0][...] = z
        if transposed:
            rows = out_refs[1].shape[-1]
            for r in range(out_refs[1].shape[0]):
                out_refs[1][r] = z[r * rows:(r + 1) * rows, :].T.astype(BF16)
        else:
            out_refs[1][...] = z.astype(BF16)
    elif mode == "gate":
        out_refs[0][...] = jax.nn.sigmoid(z).astype(BF16)
    else:
        g = g_ref[...]
        hd = g.shape[-1]
        for c in range(z.shape[1] // hd):
            sl = slice(c * hd, (c + 1) * hd)
            zc = z[:, sl]
            y = zc * lax.rsqrt(jnp.mean(zc * zc, axis=-1, keepdims=True) + EPS) * g
            if mode == "k":
                out_refs[0][:, c, :] = y
                out_refs[1][:, sl] = y.astype(BF16)
            elif transposed:
                rows = out_refs[0].shape[-1]
                for r in range(out_refs[0].shape[0]):
                    out_refs[0][r, sl, :] = (y[r * rows:(r + 1) * rows, :] * scale).T.astype(BF16)
            else:
                out_refs[0][:, sl] = (y * scale).astype(BF16)


def _proj(h, w, n, mode, gain=None, scale=1.0, transposed_tile=0, side_jobs=()):
    m, k = h.shape
    two_out = mode in ("k", "v")
    transposed = transposed_tile > 0
    tm = _tile(m, (512,) if mode == "k" else (1024, 512, 256, 128, 64, 32, 16))
    tn = _tile(n, (512,) if mode == "v" else (1024, 512, 256, 128))
    slabs = tm // transposed_tile if transposed else 1
    if gain is None:
        gain = jnp.ones((1, 128), F32)
    gain = gain.reshape(1, -1).astype(F32)
    o_spec = pl.BlockSpec((tm, tn), lambda i, j: (i, j))
    if mode == "u":
        out_shape, out_specs = [jax.ShapeDtypeStruct((m, n), F32)], [o_spec]
    elif mode == "v" and transposed:
        out_shape = [jax.ShapeDtypeStruct((m, n), F32),
                     jax.ShapeDtypeStruct((m // transposed_tile, n, transposed_tile), BF16)]
        out_specs = [o_spec, pl.BlockSpec((slabs, tn, transposed_tile), lambda i, j: (i, j, 0))]
    elif mode == "k":
        hd = gain.shape[-1]
        out_shape = [jax.ShapeDtypeStruct((m, n // hd, hd), F32), jax.ShapeDtypeStruct((m, n), BF16)]
        out_specs = [pl.BlockSpec((tm, tn // hd, hd), lambda i, j: (i, j, 0)), o_spec]
    elif two_out:
        out_shape = [jax.ShapeDtypeStruct((m, n), F32), jax.ShapeDtypeStruct((m, n), BF16)]
        out_specs = [o_spec, o_spec]
    elif transposed:
        out_shape = [jax.ShapeDtypeStruct((m // transposed_tile, n, transposed_tile), BF16)]
        out_specs = [pl.BlockSpec((slabs, tn, transposed_tile), lambda i, j: (i, j, 0))]
    else:
        out_shape, out_specs = [jax.ShapeDtypeStruct((m, n), BF16)], [o_spec]
    grid = (m // tm, n // tn)
    srcs, side_in, side_out, side_shapes, finish = _side_casts(side_jobs, grid)
    n_main = len(out_shape)
    outs = pl.pallas_call(
        _with_side_casts(functools.partial(_proj_kernel, mode=mode, scale=scale, transposed=transposed),
                         3, len(srcs)),
        grid=grid,
        in_specs=[pl.BlockSpec((tm, k), lambda i, j: (i, 0)),
                  pl.BlockSpec((k, tn), lambda i, j: (0, j)),
                  pl.BlockSpec(gain.shape, lambda i, j: (0, 0))] + side_in,
        out_specs=out_specs + side_out,
        out_shape=out_shape + side_shapes,
        compiler_params=_params(2),
        name="proj_" + mode,
    )(h, w, gain, *srcs)
    main = tuple(outs[:n_main]) if two_out else outs[0]
    return main, finish(outs[n_main:])


def _lambda(lq1, lk1, lq2, lk2, lam_i):
    a = jnp.sum(lq1[...] * lk1[...], axis=-1, keepdims=True)
    b = jnp.sum(lq2[...] * lk2[...], axis=-1, keepdims=True)
    return jnp.exp(a) - jnp.exp(b) + lam_i


def _head_out(o0, o1, lam, sub, lam_i):
    o = o0 - lam * o1
    y = o * lax.rsqrt(jnp.mean(o * o, axis=-1, keepdims=True) + EPS) * sub
    return y * (1.0 - lam_i)


def _nt_dot(a, b):
    return lax.dot_general(a, b, (((1,), (1,)), ((), ())), preferred_element_type=F32)


def _attn_prompt_kernel(qt_ref, k_ref, vt_ref, lq1, lk1, lq2, lk2, sub_ref, o_ref, acc_sc, sa_sc, sb_sc, sc_sc,
                        *, tq, hd, lam_i):
    nq = qt_ref.shape[0]
    lam = _lambda(lq1, lk1, lq2, lk2, lam_i)

    def put_scores(buf, qi, j):
        q = qt_ref[qi]
        k = k_ref[0, pl.ds(pl.multiple_of(j * tq, tq), tq), :]
        buf[:, :tq] = jnp.dot(k[:, :hd], q[:hd, :], preferred_element_type=F32)
        buf[:, tq:] = jnp.dot(k[:, hd:], q[hd:, :], preferred_element_type=F32)

    def update(j, s, m_old, l_old):
        m_new = jnp.maximum(m_old, jnp.max(s, axis=0, keepdims=True))
        alpha = jnp.exp2(m_old - m_new)
        p = jnp.exp2(s - m_new)
        l_new = alpha * l_old + jnp.sum(p, axis=0, keepdims=True)
        acc_sc[...] = alpha * acc_sc[...] + jnp.dot(vt_ref[j], p.astype(BF16), preferred_element_type=F32)
        return m_new, l_new

    def step(qi, j, cur, nxt, carry):
        put_scores(nxt, qi, j + 1)
        return update(j, cur[...], *carry)

    put_scores(sa_sc, 0, 0)

    def query_tile(qi, _):
        acc_sc[...] = jnp.zeros(acc_sc.shape, F32)
        carry = (jnp.full((1, 2 * tq), -jnp.inf, F32), jnp.zeros((1, 2 * tq), F32))
        peel_two = jnp.logical_and(qi >= 1, qi % 2 == 0).astype(jnp.int32)
        peel_one = (qi % 2 == 1).astype(jnp.int32)
        carry = lax.fori_loop(0, peel_one, lambda _, c: step(qi, 0, sc_sc, sa_sc, c), carry)
        carry = lax.fori_loop(
            0, peel_two, lambda _, c: step(qi, 1, sb_sc, sa_sc, step(qi, 0, sc_sc, sb_sc, c)), carry)
        j0 = 1 + peel_two

        def pair(j, c):
            return step(qi, j + 1, sb_sc, sa_sc, step(qi, j, sa_sc, sb_sc, c))

        def quad(j, c):
            return pair(j + 2, pair(j, c))

        n_pairs = jnp.maximum(qi - j0, 0) // 2
        carry = lax.fori_loop(0, n_pairs % 2, lambda _, c: pair(j0, c), carry)
        j1 = j0 + 2 * (n_pairs % 2)
        carry = lax.fori_loop(0, (n_pairs // 2) % 2, lambda _, c: quad(j1, c), carry)
        j2 = j1 + 4 * ((n_pairs // 2) % 2)
        m, l = lax.fori_loop(0, n_pairs // 4, lambda jj, c: quad(j2 + 8 * jj + 4, quad(j2 + 8 * jj, c)), carry)

        put_scores(sc_sc, jnp.minimum(qi + 1, nq - 1), 0)
        q_chunk = lax.broadcasted_iota(jnp.int32, (tq, tq), 1) // CHUNK
        k_chunk = lax.broadcasted_iota(jnp.int32, (tq, tq), 0) // CHUNK
        mask = k_chunk <= q_chunk
        s = jnp.where(jnp.concatenate([mask, mask], axis=1), sa_sc[...], jnp.finfo(F32).min)
        m, l = update(qi, s, m, l)
        inv_l = 1.0 / l
        acc = acc_sc[...]
        o0 = (acc[:, :tq] * inv_l[:, :tq]).T
        o1 = (acc[:, tq:] * inv_l[:, tq:]).T
        rows = pl.ds(pl.multiple_of(qi * tq, tq), tq)
        o_ref[0, rows, :] = _head_out(o0, o1, lam, sub_ref[...], lam_i).astype(o_ref.dtype)
        return 0

    lax.fori_loop(0, nq, query_tile, 0, unroll=next(u for u in (4, 2, 1) if nq % u == 0))


def _attn_prompt(qt, k, vt, lams, subln, n_heads, lam_i):
    b, t, qcols = k.shape
    tq = qt.shape[-1]
    hd = qcols // (2 * n_heads)
    dv = vt.shape[1] // n_heads
    nq = t // tq
    vec = pl.BlockSpec((1, hd), lambda bi, h: (0, 0))
    return pl.pallas_call(
        functools.partial(_attn_prompt_kernel, tq=tq, hd=hd, lam_i=lam_i),
        grid=(b, n_heads),
        in_specs=[pl.BlockSpec((nq, 2 * hd, tq), lambda bi, h: (bi, h, 0)),
                  pl.BlockSpec((1, t, 2 * hd), lambda bi, h: (bi, 0, h)),
                  pl.BlockSpec((nq, dv, tq), lambda bi, h: (bi, h, 0)),
                  vec, vec, vec, vec,
                  pl.BlockSpec((1, dv), lambda bi, h: (0, 0))],
        out_specs=pl.BlockSpec((1, t, dv), lambda bi, h: (bi, 0, h)),
        out_shape=jax.ShapeDtypeStruct((b, t, n_heads * dv), BF16),
        scratch_shapes=[pltpu.VMEM((dv, 2 * tq), F32)] + [pltpu.VMEM((tq, 2 * tq), F32)] * 3,
        compiler_params=_params(2),
        name="attn_prompt",
    )(qt, k, vt, *lams, subln)


def _attn_cached_kernel(q_ref, kc_ref, kn_ref, vc_ref, vn_ref, lq1, lk1, lq2, lk2, sub_ref, o_ref,
                        *, hd, dv, lam_i):
    ts = q_ref.shape[1]
    past = kc_ref.shape[1]
    q_chunk = (past + lax.broadcasted_iota(jnp.int32, (ts, 1), 0)) // CHUNK
    mask_c = lax.broadcasted_iota(jnp.int32, (ts, past), 1) // CHUNK <= q_chunk
    mask_n = (past + lax.broadcasted_iota(jnp.int32, (ts, ts), 1)) // CHUNK <= q_chunk
    neg = jnp.finfo(F32).min
    lam = _lambda(lq1, lk1, lq2, lk2, lam_i)
    for h in range(vc_ref.shape[2]):
        v_cols = slice(h * dv, (h + 1) * dv)
        v_c = vc_ref[0, :, h, :].astype(BF16)
        v_n = vn_ref[0, :, v_cols]
        outs = []
        for half in range(2):
            g = 2 * h + half
            cols = slice(g * hd, (g + 1) * hd)
            q = q_ref[0, :, cols]
            sc = jnp.where(mask_c, _nt_dot(q, kc_ref[0, :, g, :].astype(BF16)), neg)
            sn = jnp.where(mask_n, _nt_dot(q, kn_ref[0, :, cols]), neg)
            m = jnp.maximum(jnp.max(sc, axis=-1, keepdims=True), jnp.max(sn, axis=-1, keepdims=True))
            pc = jnp.exp(sc - m)
            pn = jnp.exp(sn - m)
            l = jnp.sum(pc, axis=-1, keepdims=True) + jnp.sum(pn, axis=-1, keepdims=True)
            o = (jnp.dot(pc.astype(BF16), v_c, preferred_element_type=F32)
                 + jnp.dot(pn.astype(BF16), v_n, preferred_element_type=F32))
            outs.append(o * (1.0 / l))
        o_ref[0, :, v_cols] = _head_out(outs[0], outs[1], lam, sub_ref[...], lam_i).astype(o_ref.dtype)


def _attn_cached(q, k_cache, k_new, v_cache, v_new, lams, subln, lam_i):
    b, ts, _ = q.shape
    _, past, n_heads, dv = v_cache.shape
    hd = k_cache.shape[-1]
    hb = _tile(n_heads, (SUBLANES,))
    vec = pl.BlockSpec((1, hd), lambda bi, h: (0, 0))
    return pl.pallas_call(
        functools.partial(_attn_cached_kernel, hd=hd, dv=dv, lam_i=lam_i),
        grid=(b, n_heads // hb),
        in_specs=[pl.BlockSpec((1, ts, hb * 2 * hd), lambda bi, h: (bi, 0, h)),
                  pl.BlockSpec((1, past, 2 * hb, hd), lambda bi, h: (bi, 0, h, 0)),
                  pl.BlockSpec((1, ts, hb * 2 * hd), lambda bi, h: (bi, 0, h)),
                  pl.BlockSpec((1, past, hb, dv), lambda bi, h: (bi, 0, h, 0)),
                  pl.BlockSpec((1, ts, hb * dv), lambda bi, h: (bi, 0, h)),
                  vec, vec, vec, vec,
                  pl.BlockSpec((1, dv), lambda bi, h: (0, 0))],
        out_specs=pl.BlockSpec((1, ts, hb * dv), lambda bi, h: (bi, 0, h)),
        out_shape=jax.ShapeDtypeStruct((b, ts, n_heads * dv), BF16),
        compiler_params=_params(2),
        name="attn_cached",
    )(q, k_cache, k_new, v_cache, v_new, *lams, subln)


def _poolmix_kernel(u_ref, uprev_ref, hist0_ref, wp_ref, ps_ref, ga_ref, gb_ref, at_ref, o_ref, ext_sc,
                    *, tm, pos0):
    i = pl.program_id(1)
    hist = jnp.where(i == 0, hist0_ref[0], uprev_ref[0])
    ext_sc[0:HIST_ROWS, :] = hist
    ext_sc[HIST_ROWS:HIST_ROWS + tm, :] = u_ref[0]
    pos = pos0 + i * tm + lax.broadcasted_iota(jnp.int32, (tm, 1), 0)
    gin = wp_ref.shape[1]
    gout = wp_ref.shape[2]
    for g, w in enumerate(POOL_WINDOWS):
        cin = slice(g * gin, (g + 1) * gin)
        cout = slice(g * gout, (g + 1) * gout)
        u_new = ext_sc[HIST_ROWS:HIST_ROWS + tm, cin]
        win = u_new
        for j in range(1, w):
            win = win + ext_sc[HIST_ROWS - j:HIST_ROWS - j + tm, cin]
        inv_cnt = 1.0 / jnp.minimum(pos + 1, w).astype(F32)
        pooled = win * inv_cnt - u_new
        y = jnp.dot(pooled.astype(BF16), wp_ref[g], preferred_element_type=F32) * ps_ref[:, cout]
        mix = ga_ref[0, :, cout].astype(F32) * y + gb_ref[0, :, cout].astype(F32) * at_ref[0, :, cout].astype(F32)
        o_ref[0, :, cout] = mix.astype(o_ref.dtype)


def _poolmix(u, hist0, w_pool, pool_scale, gates, attn, pos0):
    b, t, pin = u.shape
    d = attn.shape[-1]
    tm = _tile(t, (256, 128, 64, 32, 16))
    hb = tm // HIST_ROWS
    return pl.pallas_call(
        functools.partial(_poolmix_kernel, tm=tm, pos0=pos0),
        grid=(b, t // tm),
        in_specs=[pl.BlockSpec((1, tm, pin), lambda bi, i: (bi, i, 0)),
                  pl.BlockSpec((1, HIST_ROWS, pin), lambda bi, i: (bi, jnp.maximum(i * hb - 1, 0), 0)),
                  pl.BlockSpec((1, HIST_ROWS, pin), lambda bi, i: (bi, 0, 0)),
                  pl.BlockSpec(w_pool.shape, lambda bi, i: (0, 0, 0)),
                  pl.BlockSpec((1, d), lambda bi, i: (0, 0)),
                  pl.BlockSpec((1, tm, d), lambda bi, i: (bi, i, 0)),
                  pl.BlockSpec((1, tm, d), lambda bi, i: (bi, i, 1)),
                  pl.BlockSpec((1, tm, d), lambda bi, i: (bi, i, 0))],
        out_specs=pl.BlockSpec((1, tm, d), lambda bi, i: (bi, i, 0)),
        out_shape=jax.ShapeDtypeStruct((b, t, d), BF16),
        scratch_shapes=[pltpu.VMEM((HIST_ROWS + tm, pin), F32)],
        compiler_params=_params(2),
        name="poolmix",
    )(u, u, hist0, w_pool, pool_scale.reshape(1, d).astype(F32), gates, gates, attn)


def _mm_res_kernel(a_ref, w_ref, r_ref, o_ref):
    o_ref[...] = r_ref[...] + jnp.dot(a_ref[...], w_ref[...], preferred_element_type=F32)


def _mm_res(a, w, res):
    m, k = a.shape
    n = w.shape[1]
    big_k = k > 8192
    tm = _tile(m, (512,) if big_k else (1024, 512, 256, 128, 64, 32, 16))
    tn = _tile(n, (512, 256) if big_k else (1024, 512, 256, 128))
    return pl.pallas_call(
        _mm_res_kernel,
        grid=(m // tm, n // tn),
        in_specs=[pl.BlockSpec((tm, k), lambda i, j: (i, 0)),
                  pl.BlockSpec((k, tn), lambda i, j: (0, j)),
                  pl.BlockSpec((tm, tn), lambda i, j: (i, j))],
        out_specs=pl.BlockSpec((tm, tn), lambda i, j: (i, j)),
        out_shape=jax.ShapeDtypeStruct((m, n), F32),
        compiler_params=_params(2),
        name="mm_res",
    )(a, w, res)


def _swiglu_kernel(a_ref, wg_ref, wu_ref, o_ref):
    a = a_ref[...]
    gate = jnp.dot(a, wg_ref[...], preferred_element_type=F32)
    up = jnp.dot(a, wu_ref[...], preferred_element_type=F32)
    o_ref[...] = (gate * jax.nn.sigmoid(gate) * up).astype(o_ref.dtype)


def _swiglu(h, wg, wu, side_jobs=()):
    m, k = h.shape
    d_ff = wg.shape[1]
    tm = _tile(m, (2048, 1024, 512, 256, 128, 64, 32, 16))
    tn = _tile(d_ff, (256, 128))
    grid = (m // tm, d_ff // tn)
    srcs, side_in, side_out, side_shapes, finish = _side_casts(side_jobs, grid)
    w_spec = pl.BlockSpec((k, tn), lambda i, j: (0, j))
    outs = pl.pallas_call(
        _with_side_casts(_swiglu_kernel, 3, len(srcs)),
        grid=grid,
        in_specs=[pl.BlockSpec((tm, k), lambda i, j: (i, 0)), w_spec, w_spec] + side_in,
        out_specs=[pl.BlockSpec((tm, tn), lambda i, j: (i, j))] + side_out,
        out_shape=[jax.ShapeDtypeStruct((m, d_ff), BF16)] + side_shapes,
        compiler_params=_params(2),
        name="swiglu",
    )(h, wg, wu, *srcs)
    return outs[0], finish(outs[1:])


def _layer(x, cache, layer, p, wts):
    b, t, d = x.shape
    n_heads, hd, dv = p["n_heads"], p["hd"], p["dv"]
    pool_in = p["w_pool"].shape[0] * p["w_pool"].shape[1]
    attn_qk = n_heads * 2 * hd
    attn_v = n_heads * dv
    d_ff = p["w_ffn_out"].shape[0]
    o1 = pool_in
    o2 = o1 + attn_qk
    o3 = o2 + attn_qk
    o4 = o3 + attn_v
    lam_i = _lambda_init(layer)
    x2 = x.reshape(b * t, d)
    w_in, w_ffn_in = p["w_in"], p["w_ffn_in"]

    def jobs(*specs):
        todo = [sp for sp in specs if sp[0] not in wts]
        return [sp[0] for sp in todo], [sp[1:] for sp in todo]

    def keep(names, cast):
        wts.update(zip(names, cast))

    if "u" not in wts:
        wts["u"] = w_in[:, :o1].astype(BF16)
        wts["pool"] = p["w_pool"].astype(BF16)
    h = _rmsnorm(x2, p["norm_mix"])
    names, side = jobs(("k", w_in, o2, attn_qk))
    u, cast = _proj(h, wts["u"], pool_in, "u", side_jobs=side)
    keep(names, cast)
    names, side = jobs(("gate", w_in, o4, 2 * d))
    (k32, k16), cast = _proj(h, wts["k"], attn_qk, "k", gain=p["k_norm"], side_jobs=side)
    keep(names, cast)
    names, side = jobs(("q", w_in, o1, attn_qk), ("v", w_in, o3, attn_v), ("out", p["w_out"], 0, d))
    gates, cast = _proj(h, wts["gate"], 2 * d, "gate", side_jobs=side)
    keep(names, cast)
    u3 = u.reshape(b, t, pool_in)
    k3 = k16.reshape(b, t, attn_qk)
    names_q, side_q = jobs(("ffn_gate", w_ffn_in, 0, d_ff))
    names_v, side_v = jobs(("ffn_up", w_ffn_in, d_ff, d_ff))
    if cache is None:
        pos0 = 0
        hist = jnp.zeros((b, POOL_HIST, pool_in), F32)
        tq = _tile(t, (512, 256, 128, 64))
        qt, cast = _proj(h, wts["q"], attn_qk, "q", gain=p["q_norm"], scale=hd ** -0.5 * math.log2(math.e),
                         transposed_tile=tq, side_jobs=side_q)
        keep(names_q, cast)
        (v32, vt), cast = _proj(h, wts["v"], attn_v, "v", transposed_tile=tq, side_jobs=side_v)
        keep(names_v, cast)
        attn = _attn_prompt(qt, k3, vt, p["lams"], p["subln"], n_heads, lam_i)
    else:
        cache_pool, cache_k, cache_v = cache
        pos0 = cache_k.shape[1]
        hist = cache_pool
        q, cast = _proj(h, wts["q"], attn_qk, "q", gain=p["q_norm"], scale=hd ** -0.5, side_jobs=side_q)
        keep(names_q, cast)
        (v32, v16), cast = _proj(h, wts["v"], attn_v, "v", side_jobs=side_v)
        keep(names_v, cast)
        attn = _attn_cached(q.reshape(b, t, attn_qk), cache_k.reshape(b, pos0, 2 * n_heads, hd), k3,
                            cache_v, v16.reshape(b, t, attn_v), p["lams"], p["subln"], lam_i)
    hist0 = jnp.pad(hist, ((0, 0), (HIST_ROWS - POOL_HIST, 0), (0, 0)))
    mix = _poolmix(u3, hist0, wts["pool"], p["pool_scale"], gates.reshape(b, t, 2 * d), attn, pos0)

    x1 = _mm_res(mix.reshape(b * t, d), wts["out"], x2)
    h2 = _rmsnorm(x1, p["norm_ffn"])
    names, side = jobs(("ffn_out", p["w_ffn_out"], 0, d))
    act, cast = _swiglu(h2, wts["ffn_gate"], wts["ffn_up"], side_jobs=side)
    keep(names, cast)
    y = _mm_res(act, wts["ffn_out"], x1)

    new_pool = jnp.concatenate([hist, u3], axis=1)[:, -POOL_HIST:] if t < POOL_HIST else u3[:, t - POOL_HIST:]
    new_k = k32.reshape(b, t, n_heads, 2, hd)
    new_v = v32.reshape(b, t, n_heads, dv)
    return y.reshape(b, t, d), new_pool, new_k, new_v


def kernel(x_prompt, x_sample, cache_pool, cache_k, cache_v, norm_mix, w_in, w_pool, pool_scale, q_norm, k_norm,
           lambda_q1, lambda_k1, lambda_q2, lambda_k2, subln, w_out, norm_ffn, w_ffn_in, w_ffn_out):
    depth = w_in.shape[0]
    n_heads, hd, dv = cache_k.shape[3], cache_k.shape[5], cache_v.shape[4]
    xp, xs = x_prompt, x_sample
    outs_p, outs_s = [], []
    for l in range(depth):
        vec = lambda a: a[l].reshape(1, -1).astype(F32)
        p = dict(
            n_heads=n_heads, hd=hd, dv=dv,
            norm_mix=norm_mix[l], norm_ffn=norm_ffn[l], q_norm=q_norm[l], k_norm=k_norm[l],
            pool_scale=pool_scale[l], subln=vec(subln),
            lams=(vec(lambda_q1), vec(lambda_k1), vec(lambda_q2), vec(lambda_k2)),
            w_in=w_in[l], w_pool=w_pool[l], w_out=w_out[l], w_ffn_in=w_ffn_in[l], w_ffn_out=w_ffn_out[l],
        )
        wts = {}
        xp, pool_p, k_p, v_p = _layer(xp, None, l, p, wts)
        xs, pool_s, k_s, v_s = _layer(xs, (cache_pool[l], cache_k[l], cache_v[l]), l, p, wts)
        outs_p.append((pool_p, k_p, v_p))
        outs_s.append((pool_s, k_s, v_s))
    stack = lambda outs, i: outs[0][i][None] if depth == 1 else jnp.stack([o[i] for o in outs])
    return (xp, xs, stack(outs_p, 0), stack(outs_p, 1), stack(outs_p, 2),
            stack(outs_s, 0), stack(outs_s, 1), stack(outs_s, 2))
```
